```python
import jax, jax.numpy as jnp
from jax import lax
import numpy as np

D_MODEL = 1024
BATCH = 8
SEQ = 4096
DEPTH = 1

CHUNK = 64
Q_BLOCK = 128
EPS = 1e-6
DN_HEADS = 8
DN_DK = 128
DN_DV = 128
DN_CONV = 4
MLA_HEADS = 8
MLA_NOPE = 128
MLA_ROPE = 64
MLA_DV = 128
MLA_KV_RANK = 256
ROPE_THETA = 10000.0
D_FF = 2816
FFN_CONV = 3

DN_QK = DN_HEADS * DN_DK
DN_V = DN_HEADS * DN_DV
MLA_QD = MLA_HEADS * (MLA_NOPE + MLA_ROPE)
MLA_VD = MLA_HEADS * MLA_DV
IN_SPLITS = (DN_QK, DN_QK, DN_V, DN_V, DN_HEADS, DN_HEADS, MLA_QD, MLA_KV_RANK, MLA_ROPE, D_MODEL, D_MODEL)
D_IN = DN_QK * 2 + DN_V * 2 + DN_HEADS * 2 + MLA_QD + MLA_KV_RANK + MLA_ROPE + 2 * D_MODEL

kernel_name = "hybrid_gdn_mla_convffn_adaln_block"


def rms_norm(x, w):
    xf = x.astype(jnp.float32)
    y = xf * lax.rsqrt(jnp.mean(xf * xf, axis=-1, keepdims=True) + EPS)
    return (y * w.astype(jnp.float32)).astype(x.dtype)


def l2_norm(x):
    xf = x.astype(jnp.float32)
    return xf * lax.rsqrt(jnp.sum(xf * xf, axis=-1, keepdims=True) + EPS)


def causal_dwconv(x, w, b=None):
    k = w.shape[0]
    y = lax.conv_general_dilated(x, w[:, None, :].astype(x.dtype), window_strides=(1,),
                                 padding=((k - 1, 0),),
                                 dimension_numbers=('NWC', 'WIO', 'NWC'),
                                 feature_group_count=x.shape[-1])
    if b is not None:
        y = y + b
    return y


def apply_rope(x, pos):
    half = x.shape[-1] // 2
    inv = ROPE_THETA ** (-jnp.arange(half, dtype=jnp.float32) / half)
    ang = pos.astype(jnp.float32)[..., None] * inv
    cos = jnp.cos(ang)[:, :, None, :]
    sin = jnp.sin(ang)[:, :, None, :]
    xf = x.astype(jnp.float32)
    x1, x2 = xf[..., :half], xf[..., half:]
    return jnp.concatenate([x1 * cos - x2 * sin, x2 * cos + x1 * sin], axis=-1).astype(x.dtype)


def gated_delta_rule(q, k, v, g, beta):
    out_dtype = v.dtype
    B, S, H, dk = q.shape
    dv = v.shape[-1]
    n = S // CHUNK
    f32 = jnp.float32

    def chunks(t):
        return t.astype(f32).reshape(B, n, CHUNK, H, t.shape[-1]).transpose(0, 3, 1, 2, 4)

    def chunks_s(t):
        return t.astype(f32).reshape(B, n, CHUNK, H).transpose(0, 3, 1, 2)

    qc = chunks(q) * (dk ** -0.5)
    kc = chunks(k)
    vc = chunks(v)
    bc = chunks_s(beta)
    gc = jnp.cumsum(chunks_s(g), axis=-1)
    tri = jnp.tril(jnp.ones((CHUNK, CHUNK), dtype=bool))
    strict = jnp.tril(jnp.ones((CHUNK, CHUNK), dtype=bool), -1)
    decay = jnp.exp(jnp.where(tri, gc[..., :, None] - gc[..., None, :], -jnp.inf))
    k_beta = kc * bc[..., None]
    v_beta = vc * bc[..., None]
    a_mat = jnp.where(strict, jnp.einsum('bhncd,bhnjd->bhncj', k_beta, kc) * decay, 0.0)
    eye = jnp.eye(CHUNK, dtype=f32)
    rhs = jnp.concatenate([v_beta, k_beta * jnp.exp(gc)[..., None]], axis=-1)
    sol = lax.linalg.triangular_solve(eye + a_mat, rhs, left_side=True, lower=True, unit_diagonal=True)
    u, w = sol[..., :dv], sol[..., dv:]
    intra = jnp.where(tri, jnp.einsum('bhncd,bhnjd->bhncj', qc, kc) * decay, 0.0)
    q_e = qc * jnp.exp(gc)[..., None]
    k_dec = kc * jnp.exp(gc[..., -1:] - gc)[..., None]
    chunk_dec = jnp.exp(gc[..., -1])

    def to_n(t):
        return jnp.moveaxis(t, 2, 0)

    xs = (to_n(q_e), to_n(intra), to_n(u), to_n(w), to_n(k_dec), jnp.moveaxis(chunk_dec, 2, 0))

    def step(state, inp):
        qe_i, att_i, u_i, w_i, kd_i, dec_i = inp
        v_new = u_i - jnp.einsum('bhcd,bhde->bhce', w_i, state)
        o_i = jnp.einsum('bhcd,bhde->bhce', qe_i, state) + jnp.einsum('bhcj,bhje->bhce', att_i, v_new)
        state = state * dec_i[..., None, None] + jnp.einsum('bhcd,bhce->bhde', kd_i, v_new)
        return state, o_i

    s0 = jnp.zeros((B, H, dk, dv), f32)
    _, o = lax.scan(step, s0, xs)
    o = o.transpose(1, 0, 3, 2, 4).reshape(B, S, H, dv)
    return o.astype(out_dtype)


def mla_attention(q, c_kv, k_rope, pos, kv_norm_w, w_uk, w_uv, q_norm_w, k_norm_w):
    B, S, _ = q.shape
    q = q.reshape(B, S, MLA_HEADS, MLA_NOPE + MLA_ROPE)
    ckv = rms_norm(c_kv, kv_norm_w)
    k_nope = (ckv @ w_uk).reshape(B, S, MLA_HEADS, MLA_NOPE)
    v = (ckv @ w_uv).reshape(B, S, MLA_HEADS, MLA_DV)
    k = jnp.concatenate([k_nope, jnp.broadcast_to(k_rope[:, :, None, :], (B, S, MLA_HEADS, MLA_ROPE))], axis=-1)
    q = rms_norm(q, q_norm_w)
    k = rms_norm(k, k_norm_w)
    q = jnp.concatenate([q[..., :MLA_NOPE], apply_rope(q[..., MLA_NOPE:], pos)], axis=-1)
    k = jnp.concatenate([k[..., :MLA_NOPE], apply_rope(k[..., MLA_NOPE:], pos)], axis=-1)
    scale = (MLA_NOPE + MLA_ROPE) ** -0.5
    chunk_id = jnp.arange(S) // CHUNK
    outs = []
    for s0 in range(0, S, Q_BLOCK):
        s1 = s0 + Q_BLOCK
        kb, vb = k[:, :s1], v[:, :s1]
        sc = jnp.einsum('bqhd,bkhd->bhqk', q[:, s0:s1], kb, preferred_element_type=jnp.float32) * scale
        mask = chunk_id[s0:s1, None] >= chunk_id[None, :s1]
        sc = jnp.where(mask, sc, -jnp.inf)
        p = jax.nn.softmax(sc, axis=-1).astype(v.dtype)
        outs.append(jnp.einsum('bhqk,bkhd->bqhd', p, vb))
    return jnp.concatenate(outs, axis=1)


def setup_inputs(seed: int = 0) -> dict:
    key = jax.random.key(seed)
    ks = jax.random.split(key, 32)
    L = DEPTH
    f32 = jnp.float32

    def w(k, shape, fan_in, mult=1.0):
        return jax.random.normal(k, shape, f32) * (mult * fan_in ** -0.5)

    def gain(k, shape):
        return 1.0 + 0.02 * jax.random.normal(k, shape, f32)

    x = jax.random.normal(ks[0], (BATCH, SEQ, D_MODEL), f32)
    c = jax.random.normal(ks[1], (BATCH, D_MODEL), f32)
    offset = jax.random.randint(ks[2], (BATCH, 1), 0, 8192, dtype=jnp.int32)
    positions = (offset + jnp.arange(SEQ, dtype=jnp.int32)[None, :]).astype(jnp.int32)
    a_init = jax.random.uniform(ks[3], (L, DN_HEADS), f32, 1.0, 16.0)
    dt = jnp.exp(jax.random.uniform(ks[4], (L, DN_HEADS), f32, np.log(1e-3), np.log(1e-1)))
    dt_bias = dt + jnp.log(-jnp.expm1(-dt))
    return {
        "x": x,
        "c": c,
        "positions": positions,
        "w_ada": w(ks[5], (L, D_MODEL, 6 * D_MODEL), D_MODEL, 0.5),
        "b_ada": 0.02 * jax.random.normal(ks[6], (L, 6 * D_MODEL), f32),
        "norm1_w": gain(ks[7], (L, D_MODEL)),
        "w_in": w(ks[8], (L, D_MODEL, D_IN), D_MODEL),
        "dn_conv_w": w(ks[9], (L, DN_CONV, 2 * DN_QK + DN_V), DN_CONV),
        "dn_a_log": jnp.log(a_init),
        "dn_dt_bias": dt_bias,
        "dn_norm_w": gain(ks[10], (L, DN_DV)),
        "mla_kv_norm_w": gain(ks[11], (L, MLA_KV_RANK)),
        "mla_w_uk": w(ks[12], (L, MLA_KV_RANK, MLA_HEADS * MLA_NOPE), MLA_KV_RANK),
        "mla_w_uv": w(ks[13], (L, MLA_KV_RANK, MLA_VD), MLA_KV_RANK),
        "mla_q_norm_w": gain(ks[14], (L, MLA_NOPE + MLA_ROPE)),
        "mla_k_norm_w": gain(ks[15], (L, MLA_NOPE + MLA_ROPE)),
        "w_out_dn": w(ks[16], (L, DN_V, D_MODEL), DN_V),
        "w_out_mla": w(ks[17], (L, MLA_VD, D_MODEL), MLA_VD),
        "w_o": w(ks[18], (L, D_MODEL, D_MODEL), D_MODEL),
        "norm2_w": gain(ks[19], (L, D_MODEL)),
        "ffn_w_up": w(ks[20], (L, D_MODEL, 2 * D_FF), D_MODEL),
        "ffn_conv_w": w(ks[21], (L, FFN_CONV, D_FF), FFN_CONV),
        "ffn_conv_b": 0.02 * jax.random.normal(ks[22], (L, D_FF), f32),
        "ffn_w_down": w(ks[23], (L, D_FF, D_MODEL), D_FF),
    }


def reference(x, c, positions, w_ada, b_ada, norm1_w, w_in, dn_conv_w, dn_a_log, dn_dt_bias,
              dn_norm_w, mla_kv_norm_w, mla_w_uk, mla_w_uv, mla_q_norm_w, mla_k_norm_w,
              w_out_dn, w_out_mla, w_o, norm2_w, ffn_w_up, ffn_conv_w, ffn_conv_b, ffn_w_down):
    B, S, D = x.shape
    split_at = [int(o) for o in np.cumsum(IN_SPLITS)[:-1]]
    for l in range(DEPTH):
        mod = jax.nn.silu(c) @ w_ada[l] + b_ada[l]
        shift1, scale1, gate1, shift2, scale2, gate2 = jnp.split(mod[:, None, :], 6, axis=-1)

        h = rms_norm(x, norm1_w[l]) * (1 + scale1) + shift1
        proj = h @ w_in[l]
        (dn_q, dn_k, dn_v, dn_z, dn_alpha, dn_beta,
         mla_q, mla_ckv, mla_kr, gate_dn, gate_mla) = jnp.split(proj, split_at, axis=-1)

        qkv = jax.nn.silu(causal_dwconv(jnp.concatenate([dn_q, dn_k, dn_v], axis=-1), dn_conv_w[l]))
        q_a = l2_norm(qkv[..., :DN_QK].reshape(B, S, DN_HEADS, DN_DK))
        k_a = l2_norm(qkv[..., DN_QK:2 * DN_QK].reshape(B, S, DN_HEADS, DN_DK))
        v_a = qkv[..., 2 * DN_QK:].reshape(B, S, DN_HEADS, DN_DV)
        g_a = -jnp.exp(dn_a_log[l].astype(jnp.float32)) * jax.nn.softplus(
            dn_alpha.astype(jnp.float32) + dn_dt_bias[l].astype(jnp.float32))
        beta_a = jax.nn.sigmoid(dn_beta.astype(jnp.float32))
        o_a = gated_delta_rule(q_a, k_a, v_a, g_a, beta_a)
        o_a = rms_norm(o_a, dn_norm_w[l]) * jax.nn.silu(dn_z.reshape(B, S, DN_HEADS, DN_DV))
        y_a = o_a.reshape(B, S, DN_V) @ w_out_dn[l]

        o_b = mla_attention(mla_q, mla_ckv, mla_kr, positions, mla_kv_norm_w[l], mla_w_uk[l],
                            mla_w_uv[l], mla_q_norm_w[l], mla_k_norm_w[l])
        y_b = o_b.reshape(B, S, MLA_VD) @ w_out_mla[l]

        mix = jax.nn.sigmoid(gate_dn) * y_a + jax.nn.sigmoid(gate_mla) * y_b
        x = x + gate1 * (mix @ w_o[l])

        h2 = rms_norm(x, norm2_w[l]) * (1 + scale2) + shift2
        up = h2 @ ffn_w_up[l]
        a_path, v_path = up[..., :D_FF], up[..., D_FF:]
        a_path = jax.nn.gelu(causal_dwconv(a_path, ffn_conv_w[l], ffn_conv_b[l]), approximate=False)
        x = x + gate2 * ((a_path * v_path) @ ffn_w_down[l])
    return x
```

```python
import functools

import jax
import jax.numpy as jnp
import numpy as np
from jax import lax
from jax.experimental import pallas as pl
from jax.experimental.pallas import tpu as pltpu

f32 = jnp.float32
bf16 = jnp.bfloat16

D = 1024
EPS = 1e-6
CHUNK = 64
H = 8
DK = 128
NOPE, ROPE, DV = 128, 64, 128
QK_DIM = NOPE + ROPE
KV_RANK = 256
ROPE_THETA = 10000.0
D_FF = 2816
DN_CONV = 4
FFN_CONV = 3

LANES = 128
SUBLANES = 8
VMEM_LIMIT = 56 * 1024 * 1024

QPAD = 2 * LANES
P_DNQKV = 0
P_Z = 3 * D
P_MQ = 4 * D
P_GDN = 6 * D
P_GMLA = 7 * D
P_CKV = 8 * D
P_KR = P_CKV + KV_RANK
P_AB = P_KR + LANES
NP = P_AB + LANES

DN_C = 128
TM_IN, TN_IN = 1024, 2176
TM_PREP = 512
TQ = TK = 512
TM_MERGE = 512
TM_FFN, TF_FFN = 512, 1408


def _cparams(n_axes):
    return pltpu.CompilerParams(dimension_semantics=("arbitrary",) * n_axes,
                                vmem_limit_bytes=VMEM_LIMIT)


def _sigmoid(x):
    return 1.0 / (1.0 + jnp.exp(-x))


def _silu(x):
    return x * _sigmoid(x)


def _dot(a, b):
    return jnp.dot(a, b, preferred_element_type=f32)


def _dot_nt(a, b):
    return lax.dot_general(a, b, (((1,), (1,)), ((), ())), preferred_element_type=f32)


def _dot_tn(a, b):
    return lax.dot_general(a, b, (((0,), (0,)), ((), ())), preferred_element_type=f32)


def _mod_kernel(c_ref, w_ref, b_ref, o_ref):
    a = _silu(c_ref[...]).astype(bf16)
    o_ref[...] = _dot(a, w_ref[...].astype(bf16)) + b_ref[...]


def _mod(c, w_ada, b_ada):
    B = c.shape[0]
    n = w_ada.shape[1]
    return pl.pallas_call(
        _mod_kernel,
        grid=(n // D,),
        in_specs=[pl.BlockSpec((B, D), lambda j: (0, 0)),
                  pl.BlockSpec((D, D), lambda j: (0, j)),
                  pl.BlockSpec((1, D), lambda j: (0, j))],
        out_specs=pl.BlockSpec((B, D), lambda j: (0, j)),
        out_shape=jax.ShapeDtypeStruct((B, n), f32),
        compiler_params=_cparams(1),
        name="mod",
    )(c, w_ada, b_ada.reshape(1, n))


def _pack_w_in(w_in):
    o = np.cumsum((0, D, D, D, D, H, H, H * QK_DIM, KV_RANK, ROPE, D, D))
    dn_qkv = w_in[:, o[0]:o[3]]
    dn_z = w_in[:, o[3]:o[4]]
    ab = w_in[:, o[4]:o[6]]
    mq = w_in[:, o[6]:o[7]].reshape(D, H, QK_DIM)
    ckv = w_in[:, o[7]:o[8]]
    kr = w_in[:, o[8]:o[9]]
    g_dn = w_in[:, o[9]:o[10]]
    g_mla = w_in[:, o[10]:o[11]]
    mq = jnp.pad(mq, ((0, 0), (0, 0), (0, QPAD - QK_DIM))).reshape(D, H * QPAD)
    kr = jnp.pad(kr, ((0, 0), (0, LANES - ROPE)))
    ab = jnp.pad(ab, ((0, 0), (0, LANES - 2 * H)))
    return jnp.concatenate([dn_qkv, dn_z, mq, g_dn, g_mla, ckv, kr, ab], axis=1).astype(bf16)


def _modulated_norm(x, nw, shift, scale):
    r = lax.rsqrt(jnp.mean(x * x, axis=-1, keepdims=True) + EPS)
    return (x * r * nw) * (1.0 + scale) + shift


def _inproj_kernel(x_ref, mod_ref, nw_ref, w_ref, o_ref, h_ref):
    @pl.when(pl.program_id(2) == 0)
    def _():
        h = _modulated_norm(x_ref[0], nw_ref[...], mod_ref[0, 0:1, :], mod_ref[0, 1:2, :])
        h_ref[...] = h.astype(bf16)

    o_ref[0] = _dot(h_ref[...], w_ref[...])


def _inproj(x, mod3, norm_w, w_p):
    B, S, _ = x.shape
    return pl.pallas_call(
        _inproj_kernel,
        grid=(B, S // TM_IN, NP // TN_IN),
        in_specs=[pl.BlockSpec((1, TM_IN, D), lambda b, i, j: (b, i, 0)),
                  pl.BlockSpec((1, 6, D), lambda b, i, j: (b, 0, 0)),
                  pl.BlockSpec((1, D), lambda b, i, j: (0, 0)),
                  pl.BlockSpec((D, TN_IN), lambda b, i, j: (0, j))],
        out_specs=pl.BlockSpec((1, TM_IN, TN_IN), lambda b, i, j: (b, i, j)),
        out_shape=jax.ShapeDtypeStruct((B, S, NP), f32),
        scratch_shapes=[pltpu.VMEM((TM_IN, D), bf16)],
        compiler_params=_cparams(3),
        name="inproj",
    )(x, mod3, norm_w, w_p)


def _cumsum_rows(x):
    n = x.shape[0]
    row = lax.broadcasted_iota(jnp.int32, x.shape, 0)
    s = 1
    while s < n:
        x = x + jnp.where(row >= s, pltpu.roll(x, s, 0), 0.0)
        s *= 2
    return x


def _pair_cols(t, ca, cb, rows):
    return jnp.concatenate([jnp.broadcast_to(t[:, ca:ca + 1], (rows, LANES)),
                            jnp.broadcast_to(t[:, cb:cb + 1], (rows, LANES))], axis=1)


def _block_diag_rows(x):
    z = jnp.zeros((x.shape[0], LANES), x.dtype)
    return jnp.concatenate([jnp.concatenate([x[:, :LANES], z], axis=1),
                            jnp.concatenate([z, x[:, LANES:]], axis=1)], axis=0)


def _unit_lower_inverse(a_bd, masks):
    eye, pair_mask, level_masks = masks
    t = jnp.where(eye, 1.0, 0.0) - jnp.where(pair_mask, a_bd, 0.0)
    for m in level_masks:
        a_off = jnp.where(m, a_bd, 0.0).astype(bf16)
        tb = t.astype(bf16)
        t = t - _dot(tb, _dot(a_off, tb).astype(bf16))
    return t


def _doubling_masks(n, block):
    r = lax.broadcasted_iota(jnp.int32, (n, n), 0)
    c = lax.broadcasted_iota(jnp.int32, (n, n), 1)
    rx = r ^ c
    eye = rx == 0
    pair_mask = (rx == 1) & ((r & 1) == 1)
    levels = []
    s = 2
    while s < block:
        levels.append((rx >= s) & (rx < 2 * s) & ((r & s) != 0))
        s *= 2
    return eye, pair_mask, levels


def _deltanet_kernel(q_ref, k_ref, v_ref, z_ref, ab_ref, cw_ref, alog_ref, dtb_ref, nw_ref,
                     o_ref, xpad_ref, state_ref):
    C = DN_C
    P2 = 2 * LANES
    c = pl.program_id(1)

    @pl.when(c == 0)
    def _():
        xpad_ref[0:SUBLANES, :] = jnp.zeros((SUBLANES, 3 * D), f32)
        state_ref[...] = jnp.zeros_like(state_ref)

    @pl.when(c > 0)
    def _():
        xpad_ref[0:SUBLANES, :] = xpad_ref[C:C + SUBLANES, :]

    xpad_ref[SUBLANES:SUBLANES + C, 0:D] = q_ref[0]
    xpad_ref[SUBLANES:SUBLANES + C, D:2 * D] = k_ref[0]
    xpad_ref[SUBLANES:SUBLANES + C, 2 * D:3 * D] = v_ref[0]

    ab = ab_ref[0]
    x = ab + dtb_ref[...]
    softplus = jnp.maximum(x, 0.0) + jnp.log1p(jnp.exp(-jnp.abs(x)))
    g = -jnp.exp(alog_ref[...]) * softplus
    beta_t = _sigmoid(ab)
    gc = _cumsum_rows(g)
    gc_t = gc.T

    row = lax.broadcasted_iota(jnp.int32, (C, P2), 0)
    col = lax.broadcasted_iota(jnp.int32, (C, P2), 1) & (LANES - 1)
    tri = row >= col
    strict = row > col
    masks = _doubling_masks(P2, C)
    r2 = lax.broadcasted_iota(jnp.int32, (P2, P2), 0)
    c2 = lax.broadcasted_iota(jnp.int32, (P2, P2), 1)
    same_head = (r2 < LANES) == (c2 < LANES)

    def conv_silu(col0):
        sl = slice(col0, col0 + P2)
        y = cw_ref[3:4, sl] * xpad_ref[8:8 + C, sl]
        for j in range(1, DN_CONV):
            y = y + cw_ref[3 - j:4 - j, sl] * xpad_ref[8 - j:8 - j + C, sl]
        return _silu(y)

    def l2n(t):
        t2 = t * t
        ra = lax.rsqrt(jnp.sum(t2[:, :LANES], axis=-1, keepdims=True) + EPS)
        rb = lax.rsqrt(jnp.sum(t2[:, LANES:], axis=-1, keepdims=True) + EPS)
        return jnp.concatenate([t[:, :LANES] * ra, t[:, LANES:] * rb], axis=1)

    for p in range(H // 2):
        ha, hb = 2 * p, 2 * p + 1
        q = l2n(conv_silu(p * P2)) * (DK ** -0.5)
        k = l2n(conv_silu(D + p * P2))
        v = conv_silu(2 * D + p * P2)

        gcb = _pair_cols(gc, ha, hb, C)
        betab = _pair_cols(beta_t, H + ha, H + hb, C)
        gcr = jnp.concatenate([jnp.broadcast_to(gc_t[ha:ha + 1, :], (C, LANES)),
                               jnp.broadcast_to(gc_t[hb:hb + 1, :], (C, LANES))], axis=1)
        decay = jnp.exp(jnp.where(tri, gcb - gcr, -1e30))
        eg = jnp.exp(gcb)
        gl = jnp.concatenate([jnp.broadcast_to(gc[C - 1:C, ha:ha + 1], (C, LANES)),
                              jnp.broadcast_to(gc[C - 1:C, hb:hb + 1], (C, LANES))], axis=1)
        kb = k * betab
        vb = v * betab

        gram = _dot_nt(jnp.concatenate([kb, q], axis=0).astype(bf16),
                       _block_diag_rows(k).astype(bf16))
        a_slab = jnp.where(strict, gram[:C] * decay, 0.0)
        intra = jnp.where(tri, gram[C:] * decay, 0.0)
        t_inv = _unit_lower_inverse(_block_diag_rows(a_slab), masks)

        kbe = kb * eg
        rhs = jnp.concatenate([jnp.concatenate([vb[:, :LANES], kbe[:, :LANES]], axis=1),
                               jnp.concatenate([vb[:, LANES:], kbe[:, LANES:]], axis=1)], axis=0)
        sol = _dot(t_inv.astype(bf16), rhs.astype(bf16))
        u = jnp.concatenate([sol[:C, :LANES], sol[C:, :LANES]], axis=1)
        w = jnp.concatenate([sol[:C, LANES:], sol[C:, LANES:]], axis=1)

        s2 = state_ref[p]
        wq = _dot(jnp.concatenate([w, q * eg], axis=0).astype(bf16), s2.astype(bf16))
        v_new = u - wq[:C]
        o = wq[C:] + _dot(intra.astype(bf16), _block_diag_rows(v_new).astype(bf16))
        k_dec = k * jnp.exp(gl - gcb)
        kv = _dot_tn(k_dec.astype(bf16), v_new.astype(bf16))
        dec_rows = jnp.concatenate([jnp.broadcast_to(jnp.exp(gc[C - 1:C, ha:ha + 1]), (LANES, P2)),
                                    jnp.broadcast_to(jnp.exp(gc[C - 1:C, hb:hb + 1]), (LANES, P2))], axis=0)
        state_ref[p] = s2 * dec_rows + jnp.where(same_head, kv, 0.0)

        o2 = o * o
        ra = lax.rsqrt(jnp.mean(o2[:, :LANES], axis=-1, keepdims=True) + EPS)
        rb = lax.rsqrt(jnp.mean(o2[:, LANES:], axis=-1, keepdims=True) + EPS)
        on = jnp.concatenate([o[:, :LANES] * ra * nw_ref[...], o[:, LANES:] * rb * nw_ref[...]], axis=1)
        zz = z_ref[0, :, p * P2:(p + 1) * P2]
        o_ref[0, :, p * P2:(p + 1) * P2] = (on * _silu(zz)).astype(o_ref.dtype)


def _deltanet(proj, conv_w, a_log, dt_bias, norm_w):
    B, S, _ = proj.shape
    C = DN_C
    pad = lambda t: jnp.pad(t.reshape(1, H), ((0, 0), (0, LANES - H)))
    col = lambda idx, width: pl.BlockSpec((1, C, width), lambda b, c: (b, c, idx))
    const = lambda shape: pl.BlockSpec(shape, lambda b, c: (0,) * len(shape))
    return pl.pallas_call(
        _deltanet_kernel,
        grid=(B, S // C),
        in_specs=[col(0, D), col(1, D), col(2, D), col(P_Z // D, D), col(P_AB // LANES, LANES),
                  const((DN_CONV, 3 * D)), const((1, LANES)), const((1, LANES)), const((1, DK))],
        out_specs=pl.BlockSpec((1, C, D), lambda b, c: (b, c, 0)),
        out_shape=jax.ShapeDtypeStruct((B, S, D), bf16),
        scratch_shapes=[pltpu.VMEM((C + SUBLANES, 3 * D), f32),
                        pltpu.VMEM((H // 2, 2 * LANES, 2 * LANES), f32)],
        compiler_params=_cparams(2),
        name="deltanet",
    )(proj, proj, proj, proj, proj, conv_w, pad(a_log), pad(dt_bias), norm_w.reshape(1, DK))


def _rope(r, cos, sin_lo, sin_hi):
    return r * cos + pltpu.roll(r, LANES - ROPE // 2, 1) * sin_lo + pltpu.roll(r, ROPE // 2, 1) * sin_hi


def _mlaprep_kernel(q_ref, ckv_ref, kr_ref, pos_ref, inv_ref, kvw_ref, qw_ref, kwn_ref, kwr_ref,
                    wuk_ref, wuv_ref, qo_ref, ko_ref, vo_ref):
    tm = q_ref.shape[1]
    ang = pos_ref[0].astype(f32) * inv_ref[...]
    lane = lax.broadcasted_iota(jnp.int32, (tm, LANES), 1)
    cos = jnp.cos(ang)
    sin = jnp.sin(ang)
    sin_lo = jnp.where(lane < ROPE // 2, -sin, 0.0)
    sin_hi = jnp.where((lane >= ROPE // 2) & (lane < ROPE), sin, 0.0)

    ckv = ckv_ref[0]
    ckv_n = (ckv * lax.rsqrt(jnp.mean(ckv * ckv, axis=-1, keepdims=True) + EPS) * kvw_ref[...]).astype(bf16)
    k_nope = _dot(ckv_n, wuk_ref[...])
    vo_ref[0] = _dot(ckv_n, wuv_ref[...]).astype(vo_ref.dtype)

    kr = kr_ref[0]
    kr_ss = jnp.sum(kr * kr, axis=-1, keepdims=True)
    kr_rot = _rope(kr * kwr_ref[...], cos, sin_lo, sin_hi)
    scale = QK_DIM ** -0.5
    for h in range(H):
        kn = k_nope[:, h * NOPE:(h + 1) * NOPE]
        rk = lax.rsqrt((jnp.sum(kn * kn, axis=-1, keepdims=True) + kr_ss) * (1.0 / QK_DIM) + EPS)
        ko_ref[0, :, h * QPAD:h * QPAD + NOPE] = (kn * rk * kwn_ref[...]).astype(ko_ref.dtype)
        ko_ref[0, :, h * QPAD + NOPE:(h + 1) * QPAD] = (kr_rot * rk).astype(ko_ref.dtype)

        qh = q_ref[0, :, h * QPAD:(h + 1) * QPAD]
        rq = lax.rsqrt(jnp.sum(qh * qh, axis=-1, keepdims=True) * (1.0 / QK_DIM) + EPS)
        qn = qh * rq * qw_ref[...]
        qo_ref[0, :, h * QPAD:h * QPAD + NOPE] = (qn[:, :NOPE] * scale).astype(qo_ref.dtype)
        qo_ref[0, :, h * QPAD + NOPE:(h + 1) * QPAD] = (
            _rope(qn[:, NOPE:], cos, sin_lo, sin_hi) * scale).astype(qo_ref.dtype)


def _mlaprep(proj, positions, kv_norm_w, w_uk, w_uv, q_norm_w, k_norm_w):
    B, S, _ = proj.shape
    tm = TM_PREP
    half = ROPE // 2
    inv = ROPE_THETA ** (-jnp.arange(half, dtype=f32) / half)
    inv_t = jnp.concatenate([inv, inv, jnp.zeros((LANES - ROPE,), f32)]).reshape(1, LANES)
    qw = jnp.pad(q_norm_w, (0, QPAD - QK_DIM)).reshape(1, QPAD)
    kwn = k_norm_w[:NOPE].reshape(1, NOPE)
    kwr = jnp.pad(k_norm_w[NOPE:], (0, LANES - ROPE)).reshape(1, LANES)
    col = lambda idx, width: pl.BlockSpec((1, tm, width), lambda b, i: (b, i, idx))
    const = lambda shape: pl.BlockSpec(shape, lambda b, i: (0,) * len(shape))
    out = lambda width: pl.BlockSpec((1, tm, width), lambda b, i: (b, i, 0))
    return pl.pallas_call(
        _mlaprep_kernel,
        grid=(B, S // tm),
        in_specs=[col(P_MQ // (H * QPAD), H * QPAD), col(P_CKV // KV_RANK, KV_RANK), col(P_KR // LANES, LANES),
                  pl.BlockSpec((1, tm, 1), lambda b, i: (b, i, 0)),
                  const((1, LANES)), const((1, KV_RANK)), const((1, QPAD)), const((1, NOPE)), const((1, LANES)),
                  const((KV_RANK, H * NOPE)), const((KV_RANK, H * DV))],
        out_specs=[out(H * QPAD), out(H * QPAD), out(H * DV)],
        out_shape=[jax.ShapeDtypeStruct((B, S, H * QPAD), bf16),
                   jax.ShapeDtypeStruct((B, S, H * QPAD), bf16),
                   jax.ShapeDtypeStruct((B, S, H * DV), bf16)],
        compiler_params=_cparams(2),
        name="mlaprep",
    )(proj, proj, proj, positions.reshape(B, S, 1), inv_t, kv_norm_w.reshape(1, KV_RANK), qw, kwn, kwr,
      w_uk.astype(bf16), w_uv.astype(bf16))


def _attn_kernel(q_ref, k_ref, v_ref, o_ref, m_ref, l_ref, acc_ref):
    i = pl.program_id(2)
    q = q_ref[0]
    m_ref[...] = jnp.full_like(m_ref, -jnp.inf)
    l_ref[...] = jnp.zeros_like(l_ref)
    acc_ref[...] = jnp.zeros_like(acc_ref)
    r = lax.broadcasted_iota(jnp.int32, (TQ, TK), 0) // CHUNK
    c = lax.broadcasted_iota(jnp.int32, (TQ, TK), 1) // CHUNK
    visible = r >= c

    def step(j, masked):
        start = pl.multiple_of(j * TK, TK)
        s = _dot_nt(q, k_ref[0, pl.ds(start, TK), :])
        if masked:
            s = jnp.where(visible, s, -jnp.inf)
        m_prev = m_ref[...]
        m_new = jnp.maximum(m_prev, jnp.max(s, axis=-1, keepdims=True))
        alpha = jnp.exp(m_prev - m_new)
        p = jnp.exp(s - m_new)
        l_ref[...] = alpha * l_ref[...] + jnp.sum(p, axis=-1, keepdims=True)
        acc_ref[...] = alpha * acc_ref[...] + _dot(p.astype(bf16), v_ref[0, pl.ds(start, TK), :])
        m_ref[...] = m_new

    def body(j, carry):
        step(j, False)
        return carry

    lax.fori_loop(0, i, body, 0)
    step(i, True)
    o_ref[0] = (acc_ref[...] / l_ref[...]).astype(o_ref.dtype)


def _attn(qp, kp, vp):
    B, S, _ = qp.shape
    return pl.pallas_call(
        _attn_kernel,
        grid=(B, H, S // TQ),
        in_specs=[pl.BlockSpec((1, TQ, QPAD), lambda b, h, i: (b, i, h)),
                  pl.BlockSpec((1, S, QPAD), lambda b, h, i: (b, 0, h)),
                  pl.BlockSpec((1, S, DV), lambda b, h, i: (b, 0, h))],
        out_specs=pl.BlockSpec((1, TQ, DV), lambda b, h, i: (b, i, h)),
        out_shape=jax.ShapeDtypeStruct((B, S, H * DV), bf16),
        scratch_shapes=[pltpu.VMEM((TQ, 1), f32), pltpu.VMEM((TQ, 1), f32), pltpu.VMEM((TQ, DV), f32)],
        compiler_params=_cparams(3),
        name="attn",
    )(qp, kp, vp)


def _merge_kernel(x_ref, oa_ref, ob_ref, gd_ref, gm_ref, mod_ref, wa_ref, wb_ref, wo_ref, o_ref):
    y_a = _dot(oa_ref[0], wa_ref[...])
    y_b = _dot(ob_ref[0], wb_ref[...])
    mix = _sigmoid(gd_ref[0]) * y_a + _sigmoid(gm_ref[0]) * y_b
    o_ref[0] = x_ref[0] + mod_ref[0, 2:3, :] * _dot(mix.astype(bf16), wo_ref[...])


def _merge(x, o_a, o_b, proj, mod3, w_a, w_b, w_o):
    B, S, _ = x.shape
    tm = TM_MERGE
    row = lambda idx: pl.BlockSpec((1, tm, D), lambda b, i: (b, i, idx))
    wspec = pl.BlockSpec((D, D), lambda b, i: (0, 0))
    return pl.pallas_call(
        _merge_kernel,
        grid=(B, S // tm),
        in_specs=[row(0), row(0), row(0), row(P_GDN // D), row(P_GMLA // D),
                  pl.BlockSpec((1, 6, D), lambda b, i: (b, 0, 0)), wspec, wspec, wspec],
        out_specs=row(0),
        out_shape=jax.ShapeDtypeStruct((B, S, D), f32),
        compiler_params=_cparams(2),
        name="merge",
    )(x, o_a, o_b, proj, proj, mod3, w_a.astype(bf16), w_b.astype(bf16), w_o.astype(bf16))


def _ffn_kernel(x_ref, halo_ref, mod_ref, nw_ref, wa_ref, wv_ref, cw_ref, cb_ref, wd_ref, o_ref,
                h_ref, a_ref, acc_ref):
    tm = x_ref.shape[1]
    i = pl.program_id(1)
    j = pl.program_id(2)

    @pl.when(j == 0)
    def _():
        shift, scale = mod_ref[0, 3:4, :], mod_ref[0, 4:5, :]
        hh = _modulated_norm(halo_ref[0], nw_ref[...], shift, scale)
        h_ref[0:SUBLANES, :] = jnp.where(i > 0, hh, 0.0).astype(bf16)
        h_ref[SUBLANES:, :] = _modulated_norm(x_ref[0], nw_ref[...], shift, scale).astype(bf16)
        acc_ref[...] = jnp.zeros_like(acc_ref)

    a_ref[...] = _dot(h_ref[...], wa_ref[...])
    v = _dot(h_ref[SUBLANES:, :], wv_ref[...])
    a = cb_ref[...] + cw_ref[2:3, :] * a_ref[8:8 + tm, :]
    for t in range(1, FFN_CONV):
        a = a + cw_ref[2 - t:3 - t, :] * a_ref[8 - t:8 - t + tm, :]
    gelu = 0.5 * a * (1.0 + lax.erf(a * (2.0 ** -0.5)))
    acc_ref[...] += _dot((gelu * v).astype(bf16), wd_ref[...])

    @pl.when(j == pl.num_programs(2) - 1)
    def _():
        o_ref[0] = x_ref[0] + mod_ref[0, 5:6, :] * acc_ref[...]


def _ffn(x, mod3, norm_w, w_up, conv_w, conv_b, w_down):
    B, S, _ = x.shape
    tm, tf = TM_FFN, TF_FFN
    nf = D_FF // tf
    hb = tm // SUBLANES
    return pl.pallas_call(
        _ffn_kernel,
        grid=(B, S // tm, nf),
        in_specs=[pl.BlockSpec((1, tm, D), lambda b, i, j: (b, i, 0)),
                  pl.BlockSpec((1, SUBLANES, D), lambda b, i, j: (b, jnp.maximum(i * hb - 1, 0), 0)),
                  pl.BlockSpec((1, 6, D), lambda b, i, j: (b, 0, 0)),
                  pl.BlockSpec((1, D), lambda b, i, j: (0, 0)),
                  pl.BlockSpec((D, tf), lambda b, i, j: (0, j)),
                  pl.BlockSpec((D, tf), lambda b, i, j: (0, nf + j)),
                  pl.BlockSpec((FFN_CONV, tf), lambda b, i, j: (0, j)),
                  pl.BlockSpec((1, tf), lambda b, i, j: (0, j)),
                  pl.BlockSpec((tf, D), lambda b, i, j: (j, 0))],
        out_specs=pl.BlockSpec((1, tm, D), lambda b, i, j: (b, i, 0)),
        out_shape=jax.ShapeDtypeStruct((B, S, D), f32),
        scratch_shapes=[pltpu.VMEM((tm + SUBLANES, D), bf16),
                        pltpu.VMEM((tm + SUBLANES, tf), f32),
                        pltpu.VMEM((tm, D), f32)],
        compiler_params=_cparams(3),
        name="ffn",
    )(x, x, mod3, norm_w, w_up, w_up, conv_w, conv_b.reshape(1, D_FF), w_down)


def kernel(x, c, positions, w_ada, b_ada, norm1_w, w_in, dn_conv_w, dn_a_log, dn_dt_bias, dn_norm_w,
           mla_kv_norm_w, mla_w_uk, mla_w_uv, mla_q_norm_w, mla_k_norm_w, w_out_dn, w_out_mla, w_o,
           norm2_w, ffn_w_up, ffn_conv_w, ffn_conv_b, ffn_w_down):
    B = x.shape[0]
    for l in range(w_ada.shape[0]):
        mod3 = _mod(c, w_ada[l], b_ada[l]).reshape(B, 6, D)
        proj = _inproj(x, mod3, norm1_w[l].reshape(1, D), _pack_w_in(w_in[l]))
        o_a = _deltanet(proj, dn_conv_w[l], dn_a_log[l], dn_dt_bias[l], dn_norm_w[l])
        qp, kp, vp = _mlaprep(proj, positions, mla_kv_norm_w[l], mla_w_uk[l], mla_w_uv[l],
                              mla_q_norm_w[l], mla_k_norm_w[l])
        o_b = _attn(qp, kp, vp)
        x = _merge(x, o_a, o_b, proj, mod3, w_out_dn[l], w_out_mla[l], w_o[l])
        x = _ffn(x, mod3, norm2_w[l].reshape(1, D), ffn_w_up[l].astype(bf16), ffn_conv_w[l],
                 ffn_conv_b[l], ffn_w_down[l].astype(bf16))
    return x
```

```python
import functools

import jax
import jax.numpy as jnp
import numpy as np
from jax import lax
from jax.experimental import pallas as pl
from jax.experimental.pallas import tpu as pltpu

f32 = jnp.float32
bf16 = jnp.bfloat16

D = 1024
EPS = 1e-6
CHUNK = 64
H = 8
DK = 128
NOPE, ROPE, DV = 128, 64, 128
QK_DIM = NOPE + ROPE
KV_RANK = 256
ROPE_THETA = 10000.0
D_FF = 2816
DN_CONV = 4
FFN_CONV = 3

LANES = 128
SUBLANES = 8
VMEM_LIMIT = 56 * 1024 * 1024

QPAD = 2 * LANES
P_DNQKV = 0
P_Z = 3 * D
P_MQ = 4 * D
P_GDN = 6 * D
P_GMLA = 7 * D
P_CKV = 8 * D
P_KR = P_CKV + KV_RANK
P_AB = P_KR + LANES
NP = P_AB + LANES

DN_C = 128
TM_IN, TN_IN = 1024, 2176
TM_PREP = 512
TQ = TK = 512
ATT_G = 2
TM_MERGE = 512
TM_FFN, TF_FFN = 512, 1408


def _cparams(n_axes):
    return pltpu.CompilerParams(dimension_semantics=("arbitrary",) * n_axes,
                                vmem_limit_bytes=VMEM_LIMIT)


def _sigmoid(x):
    return 1.0 / (1.0 + jnp.exp(-x))


def _silu(x):
    return x * _sigmoid(x)


def _dot(a, b):
    return jnp.dot(a, b, preferred_element_type=f32)


def _dot_nt(a, b):
    return lax.dot_general(a, b, (((1,), (1,)), ((), ())), preferred_element_type=f32)


def _dot_tn(a, b):
    return lax.dot_general(a, b, (((0,), (0,)), ((), ())), preferred_element_type=f32)


def _mod_kernel(c_ref, w_ref, b_ref, o_ref):
    a = _silu(c_ref[...]).astype(bf16)
    o_ref[...] = _dot(a, w_ref[...].astype(bf16)) + b_ref[...]


def _mod(c, w_ada, b_ada):
    B = c.shape[0]
    n = w_ada.shape[1]
    return pl.pallas_call(
        _mod_kernel,
        grid=(n // D,),
        in_specs=[pl.BlockSpec((B, D), lambda j: (0, 0)),
                  pl.BlockSpec((D, D), lambda j: (0, j)),
                  pl.BlockSpec((1, D), lambda j: (0, j))],
        out_specs=pl.BlockSpec((B, D), lambda j: (0, j)),
        out_shape=jax.ShapeDtypeStruct((B, n), f32),
        compiler_params=_cparams(1),
        name="mod",
    )(c, w_ada, b_ada.reshape(1, n))


def _pack_w_in(w_in):
    o = np.cumsum((0, D, D, D, D, H, H, H * QK_DIM, KV_RANK, ROPE, D, D))
    dn_qkv = w_in[:, o[0]:o[3]]
    dn_z = w_in[:, o[3]:o[4]]
    ab = w_in[:, o[4]:o[6]]
    mq = w_in[:, o[6]:o[7]].reshape(D, H, QK_DIM)
    ckv = w_in[:, o[7]:o[8]]
    kr = w_in[:, o[8]:o[9]]
    g_dn = w_in[:, o[9]:o[10]]
    g_mla = w_in[:, o[10]:o[11]]
    mq = jnp.pad(mq, ((0, 0), (0, 0), (0, QPAD - QK_DIM))).reshape(D, H * QPAD)
    kr = jnp.pad(kr, ((0, 0), (0, LANES - ROPE)))
    ab = jnp.pad(ab, ((0, 0), (0, LANES - 2 * H)))
    return jnp.concatenate([dn_qkv, dn_z, mq, g_dn, g_mla, ckv, kr, ab], axis=1).astype(bf16)


def _modulated_norm(x, nw, shift, scale):
    r = lax.rsqrt(jnp.mean(x * x, axis=-1, keepdims=True) + EPS)
    return (x * r * nw) * (1.0 + scale) + shift


def _inproj_kernel(x_ref, mod_ref, nw_ref, w_ref, o_ref, h_ref):
    @pl.when(pl.program_id(2) == 0)
    def _():
        h = _modulated_norm(x_ref[0], nw_ref[...], mod_ref[0, 0:1, :], mod_ref[0, 1:2, :])
        h_ref[...] = h.astype(bf16)

    o_ref[0] = _dot(h_ref[...], w_ref[...])


def _inproj(x, mod3, norm_w, w_p):
    B, S, _ = x.shape
    return pl.pallas_call(
        _inproj_kernel,
        grid=(B, S // TM_IN, NP // TN_IN),
        in_specs=[pl.BlockSpec((1, TM_IN, D), lambda b, i, j: (b, i, 0)),
                  pl.BlockSpec((1, 6, D), lambda b, i, j: (b, 0, 0)),
                  pl.BlockSpec((1, D), lambda b, i, j: (0, 0)),
                  pl.BlockSpec((D, TN_IN), lambda b, i, j: (0, j))],
        out_specs=pl.BlockSpec((1, TM_IN, TN_IN), lambda b, i, j: (b, i, j)),
        out_shape=jax.ShapeDtypeStruct((B, S, NP), f32),
        scratch_shapes=[pltpu.VMEM((TM_IN, D), bf16)],
        compiler_params=_cparams(3),
        name="inproj",
    )(x, mod3, norm_w, w_p)


def _cumsum_rows(x):
    n = x.shape[0]
    row = lax.broadcasted_iota(jnp.int32, x.shape, 0)
    s = 1
    while s < n:
        x = x + jnp.where(row >= s, pltpu.roll(x, s, 0), 0.0)
        s *= 2
    return x


def _pair_cols(t, ca, cb, rows):
    return jnp.concatenate([jnp.broadcast_to(t[:, ca:ca + 1], (rows, LANES)),
                            jnp.broadcast_to(t[:, cb:cb + 1], (rows, LANES))], axis=1)


def _block_diag_rows(x):
    z = jnp.zeros((x.shape[0], LANES), x.dtype)
    return jnp.concatenate([jnp.concatenate([x[:, :LANES], z], axis=1),
                            jnp.concatenate([z, x[:, LANES:]], axis=1)], axis=0)


def _doubling_masks(n, block):
    r = lax.broadcasted_iota(jnp.int32, (n, n), 0)
    c = lax.broadcasted_iota(jnp.int32, (n, n), 1)
    rx = r ^ c
    eye = rx == 0
    pair_mask = (rx == 1) & ((r & 1) == 1)
    levels = []
    s = 2
    while s < block:
        levels.append((rx >= s) & (rx < 2 * s) & ((r & s) != 0))
        s *= 2
    return eye, pair_mask, levels


def _deltanet_kernel(q_ref, k_ref, v_ref, z_ref, ab_ref, cw_ref, alog_ref, dtb_ref, nw_ref,
                     o_ref, xpad_ref, state_ref):
    C = DN_C
    P2 = 2 * LANES
    c = pl.program_id(1)

    @pl.when(c == 0)
    def _():
        xpad_ref[0:SUBLANES, :] = jnp.zeros((SUBLANES, 3 * D), f32)
        state_ref[...] = jnp.zeros_like(state_ref)

    @pl.when(c > 0)
    def _():
        xpad_ref[0:SUBLANES, :] = xpad_ref[C:C + SUBLANES, :]

    xpad_ref[SUBLANES:SUBLANES + C, 0:D] = q_ref[0]
    xpad_ref[SUBLANES:SUBLANES + C, D:2 * D] = k_ref[0]
    xpad_ref[SUBLANES:SUBLANES + C, 2 * D:3 * D] = v_ref[0]

    ab = ab_ref[0]
    x = ab + dtb_ref[...]
    softplus = jnp.maximum(x, 0.0) + jnp.log1p(jnp.exp(-jnp.abs(x)))
    g = -jnp.exp(alog_ref[...]) * softplus
    beta_t = _sigmoid(ab)
    gc = _cumsum_rows(g)
    gc_t = gc.T

    row = lax.broadcasted_iota(jnp.int32, (C, P2), 0)
    col = lax.broadcasted_iota(jnp.int32, (C, P2), 1) & (LANES - 1)
    tri = row >= col
    strict = row > col
    masks = _doubling_masks(P2, C)
    r2 = lax.broadcasted_iota(jnp.int32, (P2, P2), 0)
    c2 = lax.broadcasted_iota(jnp.int32, (P2, P2), 1)
    same_head = (r2 < LANES) == (c2 < LANES)

    def conv_silu(col0):
        sl = slice(col0, col0 + P2)
        y = cw_ref[3:4, sl] * xpad_ref[8:8 + C, sl]
        for j in range(1, DN_CONV):
            y = y + cw_ref[3 - j:4 - j, sl] * xpad_ref[8 - j:8 - j + C, sl]
        return _silu(y)

    def l2n(t):
        t2 = t * t
        ra = lax.rsqrt(jnp.sum(t2[:, :LANES], axis=-1, keepdims=True) + EPS)
        rb = lax.rsqrt(jnp.sum(t2[:, LANES:], axis=-1, keepdims=True) + EPS)
        return jnp.concatenate([t[:, :LANES] * ra, t[:, LANES:] * rb], axis=1)

    n_pairs = H // 2
    pr = []
    for p in range(n_pairs):
        ha, hb = 2 * p, 2 * p + 1
        q = l2n(conv_silu(p * P2)) * (DK ** -0.5)
        k = l2n(conv_silu(D + p * P2))
        v = conv_silu(2 * D + p * P2)
        gcb = _pair_cols(gc, ha, hb, C)
        betab = _pair_cols(beta_t, H + ha, H + hb, C)
        gcr = jnp.concatenate([jnp.broadcast_to(gc_t[ha:ha + 1, :], (C, LANES)),
                               jnp.broadcast_to(gc_t[hb:hb + 1, :], (C, LANES))], axis=1)
        decay = jnp.exp(jnp.where(tri, gcb - gcr, -1e30))
        eg = jnp.exp(gcb)
        gl = jnp.concatenate([jnp.broadcast_to(gc[C - 1:C, ha:ha + 1], (C, LANES)),
                              jnp.broadcast_to(gc[C - 1:C, hb:hb + 1], (C, LANES))], axis=1)
        kb = k * betab
        vb = v * betab
        gram = _dot_nt(jnp.concatenate([kb, q], axis=0).astype(bf16),
                       _block_diag_rows(k).astype(bf16))
        kbe = kb * eg
        rhs = jnp.concatenate([jnp.concatenate([vb[:, :LANES], kbe[:, :LANES]], axis=1),
                               jnp.concatenate([vb[:, LANES:], kbe[:, LANES:]], axis=1)], axis=0).astype(bf16)
        pr.append(dict(
            a_bd=_block_diag_rows(jnp.where(strict, gram[:C] * decay, 0.0)),
            intra=jnp.where(tri, gram[C:] * decay, 0.0).astype(bf16),
            rhs=rhs, qe=q * eg, k_dec=(k * jnp.exp(gl - gcb)).astype(bf16),
            dec_rows=jnp.concatenate(
                [jnp.broadcast_to(jnp.exp(gc[C - 1:C, ha:ha + 1]), (LANES, P2)),
                 jnp.broadcast_to(jnp.exp(gc[C - 1:C, hb:hb + 1]), (LANES, P2))], axis=0)))

    eye, pair_mask, level_masks = masks
    ts = [jnp.where(eye, 1.0, 0.0) - jnp.where(pair_mask, d["a_bd"], 0.0) for d in pr]
    for m in level_masks:
        tbs = [t.astype(bf16) for t in ts]
        inner = [_dot(jnp.where(m, d["a_bd"], 0.0).astype(bf16), tb).astype(bf16) for d, tb in zip(pr, tbs)]
        ts = [t - _dot(tb, x) for t, tb, x in zip(ts, tbs, inner)]

    sols = [_dot(t.astype(bf16), d["rhs"]) for t, d in zip(ts, pr)]
    us = [jnp.concatenate([s[:C, :LANES], s[C:, :LANES]], axis=1) for s in sols]
    ws = [jnp.concatenate([s[:C, LANES:], s[C:, LANES:]], axis=1) for s in sols]
    s2s = [state_ref[p] for p in range(n_pairs)]
    wqs = [_dot(jnp.concatenate([w, d["qe"]], axis=0).astype(bf16), s2.astype(bf16))
           for w, d, s2 in zip(ws, pr, s2s)]
    v_news = [u - wq[:C] for u, wq in zip(us, wqs)]
    outs = [wq[C:] + _dot(d["intra"], _block_diag_rows(vn).astype(bf16)) for wq, d, vn in zip(wqs, pr, v_news)]
    kvs = [_dot_tn(d["k_dec"], vn.astype(bf16)) for d, vn in zip(pr, v_news)]
    for p in range(n_pairs):
        state_ref[p] = s2s[p] * pr[p]["dec_rows"] + jnp.where(same_head, kvs[p], 0.0)
        o = outs[p]
        o2 = o * o
        ra = lax.rsqrt(jnp.mean(o2[:, :LANES], axis=-1, keepdims=True) + EPS)
        rb = lax.rsqrt(jnp.mean(o2[:, LANES:], axis=-1, keepdims=True) + EPS)
        on = jnp.concatenate([o[:, :LANES] * ra * nw_ref[...], o[:, LANES:] * rb * nw_ref[...]], axis=1)
        zz = z_ref[0, :, p * P2:(p + 1) * P2]
        o_ref[0, :, p * P2:(p + 1) * P2] = (on * _silu(zz)).astype(o_ref.dtype)


def _deltanet(proj, conv_w, a_log, dt_bias, norm_w):
    B, S, _ = proj.shape
    C = DN_C
    pad = lambda t: jnp.pad(t.reshape(1, H), ((0, 0), (0, LANES - H)))
    col = lambda idx, width: pl.BlockSpec((1, C, width), lambda b, c: (b, c, idx))
    const = lambda shape: pl.BlockSpec(shape, lambda b, c: (0,) * len(shape))
    return pl.pallas_call(
        _deltanet_kernel,
        grid=(B, S // C),
        in_specs=[col(0, D), col(1, D), col(2, D), col(P_Z // D, D), col(P_AB // LANES, LANES),
                  const((DN_CONV, 3 * D)), const((1, LANES)), const((1, LANES)), const((1, DK))],
        out_specs=pl.BlockSpec((1, C, D), lambda b, c: (b, c, 0)),
        out_shape=jax.ShapeDtypeStruct((B, S, D), bf16),
        scratch_shapes=[pltpu.VMEM((C + SUBLANES, 3 * D), f32),
                        pltpu.VMEM((H // 2, 2 * LANES, 2 * LANES), f32)],
        compiler_params=_cparams(2),
        name="deltanet",
    )(proj, proj, proj, proj, proj, conv_w, pad(a_log), pad(dt_bias), norm_w.reshape(1, DK))


def _rope(r, cos, sin_lo, sin_hi):
    return r * cos + pltpu.roll(r, LANES - ROPE // 2, 1) * sin_lo + pltpu.roll(r, ROPE // 2, 1) * sin_hi


def _mlaprep_kernel(q_ref, ckv_ref, kr_ref, pos_ref, inv_ref, kvw_ref, qw_ref, kwn_ref, kwr_ref,
                    wuk_ref, wuv_ref, qo_ref, ko_ref, vo_ref):
    tm = q_ref.shape[1]
    ang = pos_ref[0].astype(f32) * inv_ref[...]
    lane = lax.broadcasted_iota(jnp.int32, (tm, LANES), 1)
    cos = jnp.cos(ang)
    sin = jnp.sin(ang)
    sin_lo = jnp.where(lane < ROPE // 2, -sin, 0.0)
    sin_hi = jnp.where((lane >= ROPE // 2) & (lane < ROPE), sin, 0.0)

    ckv = ckv_ref[0]
    ckv_n = (ckv * lax.rsqrt(jnp.mean(ckv * ckv, axis=-1, keepdims=True) + EPS) * kvw_ref[...]).astype(bf16)
    k_nope = _dot(ckv_n, wuk_ref[...])
    vo_ref[0] = _dot(ckv_n, wuv_ref[...]).astype(vo_ref.dtype)

    kr = kr_ref[0]
    kr_ss = jnp.sum(kr * kr, axis=-1, keepdims=True)
    kr_rot = _rope(kr * kwr_ref[...], cos, sin_lo, sin_hi)
    scale = QK_DIM ** -0.5
    for h in range(H):
        kn = k_nope[:, h * NOPE:(h + 1) * NOPE]
        rk = lax.rsqrt((jnp.sum(kn * kn, axis=-1, keepdims=True) + kr_ss) * (1.0 / QK_DIM) + EPS)
        ko_ref[0, :, h * QPAD:h * QPAD + NOPE] = (kn * rk * kwn_ref[...]).astype(ko_ref.dtype)
        ko_ref[0, :, h * QPAD + NOPE:(h + 1) * QPAD] = (kr_rot * rk).astype(ko_ref.dtype)

        qh = q_ref[0, :, h * QPAD:(h + 1) * QPAD]
        rq = lax.rsqrt(jnp.sum(qh * qh, axis=-1, keepdims=True) * (1.0 / QK_DIM) + EPS)
        qn = qh * rq * qw_ref[...]
        qo_ref[0, :, h * QPAD:h * QPAD + NOPE] = (qn[:, :NOPE] * scale).astype(qo_ref.dtype)
        qo_ref[0, :, h * QPAD + NOPE:(h + 1) * QPAD] = (
            _rope(qn[:, NOPE:], cos, sin_lo, sin_hi) * scale).astype(qo_ref.dtype)


def _mlaprep(proj, positions, kv_norm_w, w_uk, w_uv, q_norm_w, k_norm_w):
    B, S, _ = proj.shape
    tm = TM_PREP
    half = ROPE // 2
    inv = ROPE_THETA ** (-jnp.arange(half, dtype=f32) / half)
    inv_t = jnp.concatenate([inv, inv, jnp.zeros((LANES - ROPE,), f32)]).reshape(1, LANES)
    qw = jnp.pad(q_norm_w, (0, QPAD - QK_DIM)).reshape(1, QPAD)
    kwn = k_norm_w[:NOPE].reshape(1, NOPE)
    kwr = jnp.pad(k_norm_w[NOPE:], (0, LANES - ROPE)).reshape(1, LANES)
    col = lambda idx, width: pl.BlockSpec((1, tm, width), lambda b, i: (b, i, idx))
    const = lambda shape: pl.BlockSpec(shape, lambda b, i: (0,) * len(shape))
    out = lambda width: pl.BlockSpec((1, tm, width), lambda b, i: (b, i, 0))
    return pl.pallas_call(
        _mlaprep_kernel,
        grid=(B, S // tm),
        in_specs=[col(P_MQ // (H * QPAD), H * QPAD), col(P_CKV // KV_RANK, KV_RANK), col(P_KR // LANES, LANES),
                  pl.BlockSpec((1, tm, 1), lambda b, i: (b, i, 0)),
                  const((1, LANES)), const((1, KV_RANK)), const((1, QPAD)), const((1, NOPE)), const((1, LANES)),
                  const((KV_RANK, H * NOPE)), const((KV_RANK, H * DV))],
        out_specs=[out(H * QPAD), out(H * QPAD), out(H * DV)],
        out_shape=[jax.ShapeDtypeStruct((B, S, H * QPAD), bf16),
                   jax.ShapeDtypeStruct((B, S, H * QPAD), bf16),
                   jax.ShapeDtypeStruct((B, S, H * DV), bf16)],
        compiler_params=_cparams(2),
        name="mlaprep",
    )(proj, proj, proj, positions.reshape(B, S, 1), inv_t, kv_norm_w.reshape(1, KV_RANK), qw, kwn, kwr,
      w_uk.astype(bf16), w_uv.astype(bf16))


def _attn_kernel(q_ref, k_ref, v_ref, o_ref, m_ref, acc_ref):
    G = ATT_G
    i = pl.program_id(2)
    m_ref[...] = jnp.full_like(m_ref, -jnp.inf)
    acc_ref[...] = jnp.zeros_like(acc_ref)
    r = lax.broadcasted_iota(jnp.int32, (TQ, TK), 0) // CHUNK
    c = lax.broadcasted_iota(jnp.int32, (TQ, TK), 1) // CHUNK
    visible = r >= c
    ones = jnp.ones((TK, DV), bf16)

    def step(j, masked):
        start = pl.multiple_of(j * TK, TK)
        scores = [_dot_nt(q_ref[0, :, g * QPAD:(g + 1) * QPAD], k_ref[0, pl.ds(start, TK), g * QPAD:(g + 1) * QPAD])
                  for g in range(G)]
        probs, alphas = [], []
        for g in range(G):
            s = jnp.where(visible, scores[g], -jnp.inf) if masked else scores[g]
            m_prev = m_ref[g]
            m_new = jnp.maximum(m_prev, jnp.max(s, axis=-1, keepdims=True))
            alphas.append(jnp.exp(m_prev - m_new))
            probs.append(jnp.exp(s - m_new).astype(bf16))
            m_ref[g] = m_new
        for g in range(G):
            v1 = jnp.concatenate([v_ref[0, pl.ds(start, TK), g * DV:(g + 1) * DV], ones], axis=1)
            acc_ref[g] = alphas[g] * acc_ref[g] + _dot(probs[g], v1)

    def body(j, carry):
        step(j, False)
        return carry

    lax.fori_loop(0, i, body, 0)
    step(i, True)
    for g in range(G):
        acc = acc_ref[g]
        o_ref[0, :, g * DV:(g + 1) * DV] = (acc[:, :DV] / acc[:, DV:]).astype(o_ref.dtype)


def _attn(qp, kp, vp):
    B, S, _ = qp.shape
    G = ATT_G
    return pl.pallas_call(
        _attn_kernel,
        grid=(B, H // G, S // TQ),
        in_specs=[pl.BlockSpec((1, TQ, G * QPAD), lambda b, h, i: (b, i, h)),
                  pl.BlockSpec((1, S, G * QPAD), lambda b, h, i: (b, 0, h)),
                  pl.BlockSpec((1, S, G * DV), lambda b, h, i: (b, 0, h))],
        out_specs=pl.BlockSpec((1, TQ, G * DV), lambda b, h, i: (b, i, h)),
        out_shape=jax.ShapeDtypeStruct((B, S, H * DV), bf16),
        scratch_shapes=[pltpu.VMEM((G, TQ, 1), f32), pltpu.VMEM((G, TQ, 2 * DV), f32)],
        compiler_params=_cparams(3),
        name="attn",
    )(qp, kp, vp)


def _merge_kernel(x_ref, oa_ref, ob_ref, gd_ref, gm_ref, mod_ref, wa_ref, wb_ref, wo_ref, o_ref):
    y_a = _dot(oa_ref[0], wa_ref[...])
    y_b = _dot(ob_ref[0], wb_ref[...])
    mix = _sigmoid(gd_ref[0]) * y_a + _sigmoid(gm_ref[0]) * y_b
    o_ref[0] = x_ref[0] + mod_ref[0, 2:3, :] * _dot(mix.astype(bf16), wo_ref[...])


def _merge(x, o_a, o_b, proj, mod3, w_a, w_b, w_o):
    B, S, _ = x.shape
    tm = TM_MERGE
    row = lambda idx: pl.BlockSpec((1, tm, D), lambda b, i: (b, i, idx))
    wspec = pl.BlockSpec((D, D), lambda b, i: (0, 0))
    return pl.pallas_call(
        _merge_kernel,
        grid=(B, S // tm),
        in_specs=[row(0), row(0), row(0), row(P_GDN // D), row(P_GMLA // D),
                  pl.BlockSpec((1, 6, D), lambda b, i: (b, 0, 0)), wspec, wspec, wspec],
        out_specs=row(0),
        out_shape=jax.ShapeDtypeStruct((B, S, D), f32),
        compiler_params=_cparams(2),
        name="merge",
    )(x, o_a, o_b, proj, proj, mod3, w_a.astype(bf16), w_b.astype(bf16), w_o.astype(bf16))


def _ffn_kernel(x_ref, halo_ref, mod_ref, nw_ref, wa_ref, wv_ref, cw_ref, cb_ref, wd_ref, o_ref,
                h_ref, a_ref, acc_ref):
    tm = x_ref.shape[1]
    i = pl.program_id(1)
    j = pl.program_id(2)

    @pl.when(j == 0)
    def _():
        shift, scale = mod_ref[0, 3:4, :], mod_ref[0, 4:5, :]
        hh = _modulated_norm(halo_ref[0], nw_ref[...], shift, scale)
        h_ref[0:SUBLANES, :] = jnp.where(i > 0, hh, 0.0).astype(bf16)
        h_ref[SUBLANES:, :] = _modulated_norm(x_ref[0], nw_ref[...], shift, scale).astype(bf16)
        acc_ref[...] = jnp.zeros_like(acc_ref)

    a_ref[...] = _dot(h_ref[...], wa_ref[...])
    v = _dot(h_ref[SUBLANES:, :], wv_ref[...])
    a = cb_ref[...] + cw_ref[2:3, :] * a_ref[8:8 + tm, :]
    for t in range(1, FFN_CONV):
        a = a + cw_ref[2 - t:3 - t, :] * a_ref[8 - t:8 - t + tm, :]
    gelu = 0.5 * a * (1.0 + lax.erf(a * (2.0 ** -0.5)))
    acc_ref[...] += _dot((gelu * v).astype(bf16), wd_ref[...])

    @pl.when(j == pl.num_programs(2) - 1)
    def _():
        o_ref[0] = x_ref[0] + mod_ref[0, 5:6, :] * acc_ref[...]


def _ffn(x, mod3, norm_w, w_up, conv_w, conv_b, w_down):
    B, S, _ = x.shape
    tm, tf = TM_FFN, TF_FFN
    nf = D_FF // tf
    hb = tm // SUBLANES
    return pl.pallas_call(
        _ffn_kernel,
        grid=(B, S // tm, nf),
        in_specs=[pl.BlockSpec((1, tm, D), lambda b, i, j: (b, i, 0)),
                  pl.BlockSpec((1, SUBLANES, D), lambda b, i, j: (b, jnp.maximum(i * hb - 1, 0), 0)),
                  pl.BlockSpec((1, 6, D), lambda b, i, j: (b, 0, 0)),
                  pl.BlockSpec((1, D), lambda b, i, j: (0, 0)),
                  pl.BlockSpec((D, tf), lambda b, i, j: (0, j)),
                  pl.BlockSpec((D, tf), lambda b, i, j: (0, nf + j)),
                  pl.BlockSpec((FFN_CONV, tf), lambda b, i, j: (0, j)),
                  pl.BlockSpec((1, tf), lambda b, i, j: (0, j)),
                  pl.BlockSpec((tf, D), lambda b, i, j: (j, 0))],
        out_specs=pl.BlockSpec((1, tm, D), lambda b, i, j: (b, i, 0)),
        out_shape=jax.ShapeDtypeStruct((B, S, D), f32),
        scratch_shapes=[pltpu.VMEM((tm + SUBLANES, D), bf16),
                        pltpu.VMEM((tm + SUBLANES, tf), f32),
                        pltpu.VMEM((tm, D), f32)],
        compiler_params=_cparams(3),
        name="ffn",
    )(x, x, mod3, norm_w, w_up, w_up, conv_w, conv_b.reshape(1, D_FF), w_down)


def kernel(x, c, positions, w_ada, b_ada, norm1_w, w_in, dn_conv_w, dn_a_log, dn_dt_bias, dn_norm_w,
           mla_kv_norm_w, mla_w_uk, mla_w_uv, mla_q_norm_w, mla_k_norm_w, w_out_dn, w_out_mla, w_o,
           norm2_w, ffn_w_up, ffn_conv_w, ffn_conv_b, ffn_w_down):
    B = x.shape[0]
    for l in range(w_ada.shape[0]):
        mod3 = _mod(c, w_ada[l], b_ada[l]).reshape(B, 6, D)
        proj = _inproj(x, mod3, norm1_w[l].reshape(1, D), _pack_w_in(w_in[l]))
        o_a = _deltanet(proj, dn_conv_w[l], dn_a_log[l], dn_dt_bias[l], dn_norm_w[l])
        qp, kp, vp = _mlaprep(proj, positions, mla_kv_norm_w[l], mla_w_uk[l], mla_w_uv[l],
                              mla_q_norm_w[l], mla_k_norm_w[l])
        o_b = _attn(qp, kp, vp)
        x = _merge(x, o_a, o_b, proj, mod3, w_out_dn[l], w_out_mla[l], w_o[l])
        x = _ffn(x, mod3, norm2_w[l].reshape(1, D), ffn_w_up[l].astype(bf16), ffn_conv_w[l],
                 ffn_conv_b[l], ffn_w_down[l].astype(bf16))
    return x
```

```python
import jax
import jax.numpy as jnp
import numpy as np
from jax import lax
from jax.experimental import pallas as pl
from jax.experimental.pallas import tpu as pltpu

f32 = jnp.float32
bf16 = jnp.bfloat16

D = 1024
EPS = 1e-6
CHUNK = 64
H = 8
DK = 128
NOPE, ROPE, DV = 128, 64, 128
QK_DIM = NOPE + ROPE
KV_RANK = 256
ROPE_THETA = 10000.0
D_FF = 2816
DN_CONV = 4
FFN_CONV = 3
LOG2E = 1.4426950408889634

LANES = 128
SUBLANES = 8
VMEM_LIMIT = 56 * 1024 * 1024

QPAD = 2 * LANES
P_DNQKV = 0
P_Z = 3 * D
P_GDN = 4 * D
P_GMLA = 5 * D
P_CKV = 6 * D
P_KR = P_CKV + KV_RANK
P_AB = P_KR + LANES
NP = P_AB + LANES

DN_C = 128
DN_NB = 2
TM_IN, TN_IN = 1024, 1664
TM_PREP = 512
TQ = 256
TK = 512
ATT_G = 4
TM_MERGE = 512
TM_FFN, TF_FFN = 512, 1408


def _cparams(n_axes):
    return pltpu.CompilerParams(dimension_semantics=("arbitrary",) * n_axes,
                                vmem_limit_bytes=VMEM_LIMIT)


def _sigmoid(x):
    return 0.5 * (1.0 + jnp.tanh(0.5 * x))


def _silu(x):
    return x * _sigmoid(x)


def _dot(a, b):
    return jnp.dot(a, b, preferred_element_type=f32)


def _dot_nt(a, b):
    return lax.dot_general(a, b, (((1,), (1,)), ((), ())), preferred_element_type=f32)


def _dot_tn(a, b):
    return lax.dot_general(a, b, (((0,), (0,)), ((), ())), preferred_element_type=f32)


def _lane_tile(t, width):
    return jnp.concatenate([t] * (width // LANES), axis=1)


def _mod_kernel(c_ref, w_ref, b_ref, o_ref):
    a = _silu(c_ref[...]).astype(bf16)
    o_ref[...] = _dot(a, w_ref[...].astype(bf16)) + b_ref[...]


def _mod(c, w_ada, b_ada):
    B = c.shape[0]
    n = w_ada.shape[1]
    return pl.pallas_call(
        _mod_kernel,
        grid=(n // D,),
        in_specs=[pl.BlockSpec((B, D), lambda j: (0, 0)),
                  pl.BlockSpec((D, D), lambda j: (0, j)),
                  pl.BlockSpec((1, D), lambda j: (0, j))],
        out_specs=pl.BlockSpec((B, D), lambda j: (0, j)),
        out_shape=jax.ShapeDtypeStruct((B, n), f32),
        compiler_params=_cparams(1),
        name="mod",
    )(c, w_ada, b_ada.reshape(1, n))


def _pack_w_in(w_in):
    o = np.cumsum((0, D, D, D, D, H, H, H * QK_DIM, KV_RANK, ROPE, D, D))
    dn_qkv = w_in[:, o[0]:o[3]]
    dn_z = w_in[:, o[3]:o[4]]
    ab = w_in[:, o[4]:o[6]]
    mq = w_in[:, o[6]:o[7]].reshape(D, H, QK_DIM)
    ckv = w_in[:, o[7]:o[8]]
    kr = w_in[:, o[8]:o[9]]
    g_dn = w_in[:, o[9]:o[10]]
    g_mla = w_in[:, o[10]:o[11]]
    mq = jnp.pad(mq, ((0, 0), (0, 0), (0, QPAD - QK_DIM))).reshape(D, H * QPAD)
    kr = jnp.pad(kr, ((0, 0), (0, LANES - ROPE)))
    ab = jnp.pad(ab, ((0, 0), (0, LANES - 2 * H)))
    w_p = jnp.concatenate([dn_qkv, dn_z, g_dn, g_mla, ckv, kr, ab], axis=1).astype(bf16)
    return w_p, mq.T.astype(bf16)


def _modulated_norm(x, nw, shift, scale):
    r = lax.rsqrt(jnp.mean(x * x, axis=-1, keepdims=True) + EPS)
    return (x * r * nw) * (1.0 + scale) + shift


def _inproj_kernel(x_ref, mod_ref, nw_ref, w_ref, wq_ref, o_ref, qt_ref, h_ref):
    @pl.when(pl.program_id(2) == 0)
    def _():
        h = _modulated_norm(x_ref[0], nw_ref[...], mod_ref[0, 0:1, :], mod_ref[0, 1:2, :])
        h_ref[...] = h.astype(bf16)
        qt_ref[0] = _dot_nt(wq_ref[...], h_ref[...]).astype(qt_ref.dtype)

    o_ref[0] = _dot(h_ref[...], w_ref[...]).astype(o_ref.dtype)


def _inproj(x, mod3, norm_w, w_p, w_qt):
    B, S, _ = x.shape
    return pl.pallas_call(
        _inproj_kernel,
        grid=(B, S // TM_IN, NP // TN_IN),
        in_specs=[pl.BlockSpec((1, TM_IN, D), lambda b, i, j: (b, i, 0)),
                  pl.BlockSpec((1, 6, D), lambda b, i, j: (b, 0, 0)),
                  pl.BlockSpec((1, D), lambda b, i, j: (0, 0)),
                  pl.BlockSpec((D, TN_IN), lambda b, i, j: (0, j)),
                  pl.BlockSpec((H * QPAD, D), lambda b, i, j: (0, 0))],
        out_specs=[pl.BlockSpec((1, TM_IN, TN_IN), lambda b, i, j: (b, i, j)),
                   pl.BlockSpec((1, H * QPAD, TM_IN), lambda b, i, j: (b, 0, i))],
        out_shape=[jax.ShapeDtypeStruct((B, S, NP), bf16),
                   jax.ShapeDtypeStruct((B, H * QPAD, S), bf16)],
        scratch_shapes=[pltpu.VMEM((TM_IN, D), bf16)],
        compiler_params=_cparams(3),
        name="inproj",
    )(x, mod3, norm_w, w_p, w_qt)


def _cumsum_rows(x):
    n = x.shape[0]
    row = lax.broadcasted_iota(jnp.int32, x.shape, 0)
    s = 1
    while s < n:
        x = x + jnp.where(row >= s, pltpu.roll(x, s, 0), 0.0)
        s *= 2
    return x


def _pair_cols(t, ca, cb, rows):
    return jnp.concatenate([jnp.broadcast_to(t[:, ca:ca + 1], (rows, LANES)),
                            jnp.broadcast_to(t[:, cb:cb + 1], (rows, LANES))], axis=1)


def _block_diag_rows(x):
    z = jnp.zeros((x.shape[0], LANES), x.dtype)
    return jnp.concatenate([jnp.concatenate([x[:, :LANES], z], axis=1),
                            jnp.concatenate([z, x[:, LANES:]], axis=1)], axis=0)


def _doubling_masks(n, block):
    r = lax.broadcasted_iota(jnp.int32, (n, n), 0)
    c = lax.broadcasted_iota(jnp.int32, (n, n), 1)
    rx = r ^ c
    eye = rx == 0
    pair_mask = (rx == 1) & ((r & 1) == 1)
    levels = []
    s = 2
    while s < block:
        levels.append((rx >= s) & (rx < 2 * s) & ((r & s) != 0))
        s *= 2
    return eye, pair_mask, levels


def _deltanet_kernel(q_ref, k_ref, v_ref, z_ref, ab_ref, cw_ref, alog_ref, dtb_ref, nw_ref,
                     o_ref, xpad_ref, state_ref):
    C = DN_C
    P2 = 2 * LANES
    c = pl.program_id(1)

    @pl.when(c == 0)
    def _():
        xpad_ref[:, 0:SUBLANES, :] = jnp.zeros((DN_NB, SUBLANES, 3 * D), f32)
        state_ref[...] = jnp.zeros_like(state_ref)

    @pl.when(c > 0)
    def _():
        xpad_ref[:, 0:SUBLANES, :] = xpad_ref[:, C:C + SUBLANES, :]

    xpad_ref[:, SUBLANES:SUBLANES + C, 0:D] = q_ref[...].astype(f32)
    xpad_ref[:, SUBLANES:SUBLANES + C, D:2 * D] = k_ref[...].astype(f32)
    xpad_ref[:, SUBLANES:SUBLANES + C, 2 * D:3 * D] = v_ref[...].astype(f32)

    row = lax.broadcasted_iota(jnp.int32, (C, P2), 0)
    col = lax.broadcasted_iota(jnp.int32, (C, P2), 1) & (LANES - 1)
    tri = row >= col
    strict = row > col
    masks = _doubling_masks(P2, C)
    r2 = lax.broadcasted_iota(jnp.int32, (P2, P2), 0)
    c2 = lax.broadcasted_iota(jnp.int32, (P2, P2), 1)
    same_head = (r2 < LANES) == (c2 < LANES)

    def conv_silu(bi, col0):
        sl = slice(col0, col0 + P2)
        y = cw_ref[3:4, sl] * xpad_ref[bi, 8:8 + C, sl]
        for j in range(1, DN_CONV):
            y = y + cw_ref[3 - j:4 - j, sl] * xpad_ref[bi, 8 - j:8 - j + C, sl]
        return _silu(y)

    def l2n(t):
        t2 = t * t
        ra = lax.rsqrt(jnp.sum(t2[:, :LANES], axis=-1, keepdims=True) + EPS)
        rb = lax.rsqrt(jnp.sum(t2[:, LANES:], axis=-1, keepdims=True) + EPS)
        return jnp.concatenate([t[:, :LANES] * ra, t[:, LANES:] * rb], axis=1)

    n_pairs = H // 2
    chains = [(bi, p) for bi in range(DN_NB) for p in range(n_pairs)]
    decays = []
    for bi in range(DN_NB):
        ab = ab_ref[bi].astype(f32)
        x = ab + dtb_ref[...]
        softplus = jnp.maximum(x, 0.0) + jnp.log1p(jnp.exp(-jnp.abs(x)))
        gc = _cumsum_rows(-jnp.exp(alog_ref[...]) * softplus)
        decays.append((gc, gc.T, _sigmoid(ab)))
    pr = []
    for bi, p in chains:
        gc, gc_t, beta_t = decays[bi]
        ha, hb = 2 * p, 2 * p + 1
        q = l2n(conv_silu(bi, p * P2)) * (DK ** -0.5)
        k = l2n(conv_silu(bi, D + p * P2))
        v = conv_silu(bi, 2 * D + p * P2)
        gcb = _pair_cols(gc, ha, hb, C)
        betab = _pair_cols(beta_t, H + ha, H + hb, C)
        gcr = jnp.concatenate([jnp.broadcast_to(gc_t[ha:ha + 1, :], (C, LANES)),
                               jnp.broadcast_to(gc_t[hb:hb + 1, :], (C, LANES))], axis=1)
        decay = jnp.exp(jnp.where(tri, gcb - gcr, -1e30))
        eg = jnp.exp(gcb)
        gl = jnp.concatenate([jnp.broadcast_to(gc[C - 1:C, ha:ha + 1], (C, LANES)),
                              jnp.broadcast_to(gc[C - 1:C, hb:hb + 1], (C, LANES))], axis=1)
        kb = k * betab
        vb = v * betab
        gram = _dot_nt(jnp.concatenate([kb, q], axis=0).astype(bf16),
                       _block_diag_rows(k).astype(bf16))
        kbe = kb * eg
        rhs = jnp.concatenate([jnp.concatenate([vb[:, :LANES], kbe[:, :LANES]], axis=1),
                               jnp.concatenate([vb[:, LANES:], kbe[:, LANES:]], axis=1)], axis=0).astype(bf16)
        pr.append(dict(
            a_bd=_block_diag_rows(jnp.where(strict, gram[:C] * decay, 0.0)),
            intra=jnp.where(tri, gram[C:] * decay, 0.0).astype(bf16),
            rhs=rhs, qe=q * eg, k_dec=(k * jnp.exp(gl - gcb)).astype(bf16),
            dec_rows=jnp.concatenate(
                [jnp.broadcast_to(jnp.exp(gc[C - 1:C, ha:ha + 1]), (LANES, P2)),
                 jnp.broadcast_to(jnp.exp(gc[C - 1:C, hb:hb + 1]), (LANES, P2))], axis=0)))

    eye, pair_mask, level_masks = masks
    ts = [jnp.where(eye, 1.0, 0.0) - jnp.where(pair_mask, d["a_bd"], 0.0) for d in pr]
    for m in level_masks:
        tbs = [t.astype(bf16) for t in ts]
        inner = [_dot(jnp.where(m, d["a_bd"], 0.0).astype(bf16), tb).astype(bf16) for d, tb in zip(pr, tbs)]
        ts = [t - _dot(tb, x) for t, tb, x in zip(ts, tbs, inner)]

    sols = [_dot(t.astype(bf16), d["rhs"]) for t, d in zip(ts, pr)]
    us = [jnp.concatenate([s[:C, :LANES], s[C:, :LANES]], axis=1) for s in sols]
    ws = [jnp.concatenate([s[:C, LANES:], s[C:, LANES:]], axis=1) for s in sols]
    s2s = [state_ref[n] for n in range(len(chains))]
    wqs = [_dot(jnp.concatenate([w, d["qe"]], axis=0).astype(bf16), s2.astype(bf16))
           for w, d, s2 in zip(ws, pr, s2s)]
    v_news = [u - wq[:C] for u, wq in zip(us, wqs)]
    outs = [wq[C:] + _dot(d["intra"], _block_diag_rows(vn).astype(bf16)) for wq, d, vn in zip(wqs, pr, v_news)]
    kvs = [_dot_tn(d["k_dec"], vn.astype(bf16)) for d, vn in zip(pr, v_news)]
    for n, (bi, p) in enumerate(chains):
        state_ref[n] = s2s[n] * pr[n]["dec_rows"] + jnp.where(same_head, kvs[n], 0.0)
        o = outs[n]
        o2 = o * o
        ra = lax.rsqrt(jnp.mean(o2[:, :LANES], axis=-1, keepdims=True) + EPS)
        rb = lax.rsqrt(jnp.mean(o2[:, LANES:], axis=-1, keepdims=True) + EPS)
        on = jnp.concatenate([o[:, :LANES] * ra * nw_ref[...], o[:, LANES:] * rb * nw_ref[...]], axis=1)
        zz = z_ref[bi, :, p * P2:(p + 1) * P2].astype(f32)
        o_ref[bi, :, p * P2:(p + 1) * P2] = (on * _silu(zz)).astype(o_ref.dtype)


def _deltanet(proj, conv_w, a_log, dt_bias, norm_w):
    B, S, _ = proj.shape
    C = DN_C
    pad = lambda t: jnp.pad(t.reshape(1, H), ((0, 0), (0, LANES - H)))
    col = lambda idx, width: pl.BlockSpec((DN_NB, C, width), lambda b, c: (b, c, idx))
    const = lambda shape: pl.BlockSpec(shape, lambda b, c: (0,) * len(shape))
    return pl.pallas_call(
        _deltanet_kernel,
        grid=(B // DN_NB, S // C),
        in_specs=[col(0, D), col(1, D), col(2, D), col(P_Z // D, D), col(P_AB // LANES, LANES),
                  const((DN_CONV, 3 * D)), const((1, LANES)), const((1, LANES)), const((1, DK))],
        out_specs=pl.BlockSpec((DN_NB, C, D), lambda b, c: (b, c, 0)),
        out_shape=jax.ShapeDtypeStruct((B, S, D), bf16),
        scratch_shapes=[pltpu.VMEM((DN_NB, C + SUBLANES, 3 * D), f32),
                        pltpu.VMEM((DN_NB * (H // 2), 2 * LANES, 2 * LANES), f32)],
        compiler_params=_cparams(2),
        name="deltanet",
    )(proj, proj, proj, proj, proj, conv_w, pad(a_log), pad(dt_bias), norm_w.reshape(1, DK))


def _rope(r, cos, sin_lo, sin_hi):
    return r * cos + pltpu.roll(r, LANES - ROPE // 2, 1) * sin_lo + pltpu.roll(r, ROPE // 2, 1) * sin_hi


def _mlaprep_kernel(qt_ref, ckv_ref, kr_ref, posc_ref, posr_ref, invr_ref, invc_ref, kvw_ref, qw_ref,
                    kwn_ref, kwr_ref, wuk_ref, wuvt_ref, qo_ref, ko_ref, vo_ref):
    tm = ckv_ref.shape[1]
    half = ROPE // 2

    ang = posc_ref[0].astype(f32) * invr_ref[...]
    lane = lax.broadcasted_iota(jnp.int32, (tm, LANES), 1)
    cos = jnp.cos(ang)
    sin = jnp.sin(ang)
    sin_lo = jnp.where(lane < half, -sin, 0.0)
    sin_hi = jnp.where((lane >= half) & (lane < ROPE), sin, 0.0)

    ckv = ckv_ref[0].astype(f32)
    ckv_n = (ckv * lax.rsqrt(jnp.mean(ckv * ckv, axis=-1, keepdims=True) + EPS) * kvw_ref[...]).astype(bf16)
    k_nope = _dot(ckv_n, wuk_ref[...])
    vo_ref[0] = _dot_nt(wuvt_ref[...], ckv_n).astype(vo_ref.dtype)

    kr = kr_ref[0].astype(f32)
    kr_ss = jnp.sum(kr * kr, axis=-1, keepdims=True)
    kr_rot = _rope(kr * kwr_ref[...], cos, sin_lo, sin_hi)
    for h in range(H):
        kn = k_nope[:, h * NOPE:(h + 1) * NOPE]
        rk = lax.rsqrt((jnp.sum(kn * kn, axis=-1, keepdims=True) + kr_ss) * (1.0 / QK_DIM) + EPS)
        ko_ref[0, :, h * QPAD:h * QPAD + NOPE] = (kn * rk * kwn_ref[...]).astype(ko_ref.dtype)
        ko_ref[0, :, h * QPAD + NOPE:(h + 1) * QPAD] = (kr_rot * rk).astype(ko_ref.dtype)

    ang_t = _lane_tile(invc_ref[...], tm) * posr_ref[0].astype(f32)
    cos_t = jnp.cos(ang_t)
    sin_t = jnp.sin(ang_t)
    qw_t = _lane_tile(qw_ref[...], tm)
    for h in range(H):
        qh = qt_ref[0, h * QPAD:(h + 1) * QPAD, :].astype(f32)
        rq = lax.rsqrt(jnp.sum(qh * qh, axis=0, keepdims=True) * (1.0 / QK_DIM) + EPS) * (QK_DIM ** -0.5 * LOG2E)
        qn = qh * rq * qw_t
        x1 = qn[NOPE:NOPE + half]
        x2 = qn[NOPE + half:QK_DIM]
        r0 = h * QPAD
        qo_ref[0, r0:r0 + NOPE, :] = qn[:NOPE].astype(qo_ref.dtype)
        qo_ref[0, r0 + NOPE:r0 + NOPE + half, :] = (x1 * cos_t - x2 * sin_t).astype(qo_ref.dtype)
        qo_ref[0, r0 + NOPE + half:r0 + QK_DIM, :] = (x2 * cos_t + x1 * sin_t).astype(qo_ref.dtype)
        qo_ref[0, r0 + QK_DIM:r0 + QPAD, :] = jnp.zeros((QPAD - QK_DIM, tm), qo_ref.dtype)


def _mlaprep(proj, q_t, positions, kv_norm_w, w_uk, w_uv, q_norm_w, k_norm_w):
    B, S, _ = proj.shape
    tm = TM_PREP
    half = ROPE // 2
    inv = ROPE_THETA ** (-jnp.arange(half, dtype=f32) / half)
    inv_r = jnp.concatenate([inv, inv, jnp.zeros((LANES - ROPE,), f32)]).reshape(1, LANES)
    inv_c = jnp.broadcast_to(inv.reshape(half, 1), (half, LANES))
    qw = jnp.broadcast_to(jnp.pad(q_norm_w, (0, QPAD - QK_DIM)).reshape(QPAD, 1), (QPAD, LANES))
    kwn = k_norm_w[:NOPE].reshape(1, NOPE)
    kwr = jnp.pad(k_norm_w[NOPE:], (0, LANES - ROPE)).reshape(1, LANES)
    col = lambda idx, width: pl.BlockSpec((1, tm, width), lambda b, i: (b, i, idx))
    const = lambda shape: pl.BlockSpec(shape, lambda b, i: (0,) * len(shape))
    return pl.pallas_call(
        _mlaprep_kernel,
        grid=(B, S // tm),
        in_specs=[pl.BlockSpec((1, H * QPAD, tm), lambda b, i: (b, 0, i)),
                  col(P_CKV // KV_RANK, KV_RANK), col(P_KR // LANES, LANES),
                  pl.BlockSpec((1, tm, 1), lambda b, i: (b, i, 0)),
                  pl.BlockSpec((1, 1, tm), lambda b, i: (b, 0, i)),
                  const((1, LANES)), const((half, LANES)), const((1, KV_RANK)), const((QPAD, LANES)),
                  const((1, NOPE)), const((1, LANES)),
                  const((KV_RANK, H * NOPE)), const((H * DV, KV_RANK))],
        out_specs=[pl.BlockSpec((1, H * QPAD, tm), lambda b, i: (b, 0, i)),
                   pl.BlockSpec((1, tm, H * QPAD), lambda b, i: (b, i, 0)),
                   pl.BlockSpec((1, H * DV, tm), lambda b, i: (b, 0, i))],
        out_shape=[jax.ShapeDtypeStruct((B, H * QPAD, S), bf16),
                   jax.ShapeDtypeStruct((B, S, H * QPAD), bf16),
                   jax.ShapeDtypeStruct((B, H * DV, S), bf16)],
        compiler_params=_cparams(2),
        name="mlaprep",
    )(q_t, proj, proj, positions.reshape(B, S, 1), positions.reshape(B, 1, S), inv_r, inv_c,
      kv_norm_w.reshape(1, KV_RANK), qw, kwn, kwr, w_uk.astype(bf16), w_uv.T.astype(bf16))


def _attn_kernel(q_ref, k_ref, v_ref, o_ref, m_ref, acc_ref):
    G = ATT_G
    i = pl.program_id(2)
    m_ref[...] = jnp.full_like(m_ref, -jnp.inf)
    acc_ref[...] = jnp.zeros_like(acc_ref)

    def step(start, nk, masked):
        scores = [_dot(k_ref[0, pl.ds(start, nk), g * QPAD:(g + 1) * QPAD], q_ref[0, g * QPAD:(g + 1) * QPAD, :])
                  for g in range(G)]
        if masked:
            kc = lax.broadcasted_iota(jnp.int32, (nk, TQ), 0) // CHUNK
            qc = lax.broadcasted_iota(jnp.int32, (nk, TQ), 1) // CHUNK
            visible = qc >= kc
        probs, alphas = [], []
        for g in range(G):
            s = jnp.where(visible, scores[g], -jnp.inf) if masked else scores[g]
            m_prev = m_ref[g]
            m_new = jnp.maximum(m_prev, jnp.max(s, axis=0, keepdims=True))
            alphas.append(jnp.exp2(m_prev - m_new))
            probs.append(jnp.exp2(s - m_new).astype(bf16))
            m_ref[g] = m_new
        ones = jnp.ones((DV, nk), bf16)
        for g in range(G):
            v1 = jnp.concatenate([v_ref[0, g * DV:(g + 1) * DV, pl.ds(start, nk)], ones], axis=0)
            acc_ref[g] = alphas[g] * acc_ref[g] + _dot(v1, probs[g])

    def body(j, carry):
        step(pl.multiple_of(j * TK, TK), TK, False)
        return carry

    lax.fori_loop(0, i // 2, body, 0)

    @pl.when(i % 2 == 1)
    def _():
        step(pl.multiple_of((i - 1) * TQ, TQ), TQ, False)

    step(pl.multiple_of(i * TQ, TQ), TQ, True)
    for g in range(G):
        acc = acc_ref[g]
        o_ref[0, :, g * DV:(g + 1) * DV] = (acc[:DV] / acc[DV:]).T.astype(o_ref.dtype)


def _attn(q_t, kp, v_t):
    B, S, _ = kp.shape
    G = ATT_G
    return pl.pallas_call(
        _attn_kernel,
        grid=(B, H // G, S // TQ),
        in_specs=[pl.BlockSpec((1, G * QPAD, TQ), lambda b, h, i: (b, h, i)),
                  pl.BlockSpec((1, S, G * QPAD), lambda b, h, i: (b, 0, h)),
                  pl.BlockSpec((1, G * DV, S), lambda b, h, i: (b, h, 0))],
        out_specs=pl.BlockSpec((1, TQ, G * DV), lambda b, h, i: (b, i, h)),
        out_shape=jax.ShapeDtypeStruct((B, S, H * DV), bf16),
        scratch_shapes=[pltpu.VMEM((G, 1, TQ), f32), pltpu.VMEM((G, 2 * DV, TQ), f32)],
        compiler_params=_cparams(3),
        name="attn",
    )(q_t, kp, v_t)


def _merge_kernel(x_ref, oa_ref, ob_ref, gd_ref, gm_ref, mod_ref, wa_ref, wb_ref, wo_ref, o_ref):
    y_a = _dot(oa_ref[0], wa_ref[...])
    y_b = _dot(ob_ref[0], wb_ref[...])
    mix = _sigmoid(gd_ref[0].astype(f32)) * y_a + _sigmoid(gm_ref[0].astype(f32)) * y_b
    o_ref[0] = x_ref[0] + mod_ref[0, 2:3, :] * _dot(mix.astype(bf16), wo_ref[...])


def _merge(x, o_a, o_b, proj, mod3, w_a, w_b, w_o):
    B, S, _ = x.shape
    tm = TM_MERGE
    row = lambda idx: pl.BlockSpec((1, tm, D), lambda b, i: (b, i, idx))
    wspec = pl.BlockSpec((D, D), lambda b, i: (0, 0))
    return pl.pallas_call(
        _merge_kernel,
        grid=(B, S // tm),
        in_specs=[row(0), row(0), row(0), row(P_GDN // D), row(P_GMLA // D),
                  pl.BlockSpec((1, 6, D), lambda b, i: (b, 0, 0)), wspec, wspec, wspec],
        out_specs=row(0),
        out_shape=jax.ShapeDtypeStruct((B, S, D), f32),
        compiler_params=_cparams(2),
        name="merge",
    )(x, o_a, o_b, proj, proj, mod3, w_a.astype(bf16), w_b.astype(bf16), w_o.astype(bf16))


def _ffn_kernel(x_ref, halo_ref, mod_ref, nw_ref, wa_ref, wv_ref, cw_ref, cb_ref, wd_ref, o_ref,
                h_ref, a_ref, acc_ref):
    tm = x_ref.shape[1]
    i = pl.program_id(1)
    j = pl.program_id(2)

    @pl.when(j == 0)
    def _():
        shift, scale = mod_ref[0, 3:4, :], mod_ref[0, 4:5, :]
        hh = _modulated_norm(halo_ref[0], nw_ref[...], shift, scale)
        h_ref[0:SUBLANES, :] = jnp.where(i > 0, hh, 0.0).astype(bf16)
        h_ref[SUBLANES:, :] = _modulated_norm(x_ref[0], nw_ref[...], shift, scale).astype(bf16)
        acc_ref[...] = jnp.zeros_like(acc_ref)

    a_ref[...] = _dot(h_ref[...], wa_ref[...])
    v = _dot(h_ref[SUBLANES:, :], wv_ref[...])
    a = cb_ref[...] + cw_ref[2:3, :] * a_ref[8:8 + tm, :]
    for t in range(1, FFN_CONV):
        a = a + cw_ref[2 - t:3 - t, :] * a_ref[8 - t:8 - t + tm, :]
    gelu = 0.5 * a * (1.0 + lax.erf(a * (2.0 ** -0.5)))
    acc_ref[...] += _dot((gelu * v).astype(bf16), wd_ref[...])

    @pl.when(j == pl.num_programs(2) - 1)
    def _():
        o_ref[0] = x_ref[0] + mod_ref[0, 5:6, :] * acc_ref[...]


def _ffn(x, mod3, norm_w, w_up, conv_w, conv_b, w_down):
    B, S, _ = x.shape
    tm, tf = TM_FFN, TF_FFN
    nf = D_FF // tf
    hb = tm // SUBLANES
    return pl.pallas_call(
        _ffn_kernel,
        grid=(B, S // tm, nf),
        in_specs=[pl.BlockSpec((1, tm, D), lambda b, i, j: (b, i, 0)),
                  pl.BlockSpec((1, SUBLANES, D), lambda b, i, j: (b, jnp.maximum(i * hb - 1, 0), 0)),
                  pl.BlockSpec((1, 6, D), lambda b, i, j: (b, 0, 0)),
                  pl.BlockSpec((1, D), lambda b, i, j: (0, 0)),
                  pl.BlockSpec((D, tf), lambda b, i, j: (0, j)),
                  pl.BlockSpec((D, tf), lambda b, i, j: (0, nf + j)),
                  pl.BlockSpec((FFN_CONV, tf), lambda b, i, j: (0, j)),
                  pl.BlockSpec((1, tf), lambda b, i, j: (0, j)),
                  pl.BlockSpec((tf, D), lambda b, i, j: (j, 0))],
        out_specs=pl.BlockSpec((1, tm, D), lambda b, i, j: (b, i, 0)),
        out_shape=jax.ShapeDtypeStruct((B, S, D), f32),
        scratch_shapes=[pltpu.VMEM((tm + SUBLANES, D), bf16),
                        pltpu.VMEM((tm + SUBLANES, tf), f32),
                        pltpu.VMEM((tm, D), f32)],
        compiler_params=_cparams(3),
        name="ffn",
    )(x, x, mod3, norm_w, w_up, w_up, conv_w, conv_b.reshape(1, D_FF), w_down)


def kernel(x, c, positions, w_ada, b_ada, norm1_w, w_in, dn_conv_w, dn_a_log, dn_dt_bias, dn_norm_w,
           mla_kv_norm_w, mla_w_uk, mla_w_uv, mla_q_norm_w, mla_k_norm_w, w_out_dn, w_out_mla, w_o,
           norm2_w, ffn_w_up, ffn_conv_w, ffn_conv_b, ffn_w_down):
    B = x.shape[0]
    for l in range(w_ada.shape[0]):
        mod3 = _mod(c, w_ada[l], b_ada[l]).reshape(B, 6, D)
        w_p, w_qt = _pack_w_in(w_in[l])
        proj, q_t = _inproj(x, mod3, norm1_w[l].reshape(1, D), w_p, w_qt)
        o_a = _deltanet(proj, dn_conv_w[l], dn_a_log[l], dn_dt_bias[l], dn_norm_w[l])
        qp_t, kp, v_t = _mlaprep(proj, q_t, positions, mla_kv_norm_w[l], mla_w_uk[l], mla_w_uv[l],
                                 mla_q_norm_w[l], mla_k_norm_w[l])
        o_b = _attn(qp_t, kp, v_t)
        x = _merge(x, o_a, o_b, proj, mod3, w_out_dn[l], w_out_mla[l], w_o[l])
        x = _ffn(x, mod3, norm2_w[l].reshape(1, D), ffn_w_up[l].astype(bf16), ffn_conv_w[l],
                 ffn_conv_b[l], ffn_w_down[l].astype(bf16))
    return x
```

```python
import jax
import jax.numpy as jnp
import numpy as np
from jax import lax
from jax.experimental import pallas as pl
from jax.experimental.pallas import tpu as pltpu

f32 = jnp.float32
bf16 = jnp.bfloat16

D = 1024
EPS = 1e-6
CHUNK = 64
H = 8
DK = 128
NOPE, ROPE, DV = 128, 64, 128
QK_DIM = NOPE + ROPE
KV_RANK = 256
ROPE_THETA = 10000.0
D_FF = 2816
DN_CONV = 4
FFN_CONV = 3
LOG2E = 1.4426950408889634

LANES = 128
SUBLANES = 8
VMEM_LIMIT = 56 * 1024 * 1024

QPAD = 2 * LANES
P_DNQKV = 0
P_Z = 3 * D
P_GDN = 4 * D
P_GMLA = 5 * D
P_CKV = 6 * D
P_KR = P_CKV + KV_RANK
P_AB = P_KR + LANES
NP = P_AB + LANES

DN_C = 128
DN_NB = 2
TM_IN, TN_IN = 1024, 1664
TM_PREP = 512
TQ = 512
TK = 512
ATT_G = 4
TM_MERGE = 512
TM_FFN, TF_FFN = 512, 1408


def _cparams(n_axes):
    return pltpu.CompilerParams(dimension_semantics=("arbitrary",) * n_axes,
                                vmem_limit_bytes=VMEM_LIMIT)


def _sigmoid(x):
    return 0.5 * (1.0 + jnp.tanh(0.5 * x))


def _silu(x):
    return x * _sigmoid(x)


def _dot(a, b):
    return jnp.dot(a, b, preferred_element_type=f32)


def _dot_nt(a, b):
    return lax.dot_general(a, b, (((1,), (1,)), ((), ())), preferred_element_type=f32)


def _dot_tn(a, b):
    return lax.dot_general(a, b, (((0,), (0,)), ((), ())), preferred_element_type=f32)


def _lane_tile(t, width):
    return jnp.concatenate([t] * (width // LANES), axis=1)


def _mod_kernel(c_ref, w_ref, b_ref, o_ref):
    a = _silu(c_ref[...]).astype(bf16)
    o_ref[...] = _dot(a, w_ref[...].astype(bf16)) + b_ref[...]


def _mod(c, w_ada, b_ada):
    B = c.shape[0]
    n = w_ada.shape[1]
    return pl.pallas_call(
        _mod_kernel,
        grid=(n // D,),
        in_specs=[pl.BlockSpec((B, D), lambda j: (0, 0)),
                  pl.BlockSpec((D, D), lambda j: (0, j)),
                  pl.BlockSpec((1, D), lambda j: (0, j))],
        out_specs=pl.BlockSpec((B, D), lambda j: (0, j)),
        out_shape=jax.ShapeDtypeStruct((B, n), f32),
        compiler_params=_cparams(1),
        name="mod",
    )(c, w_ada, b_ada.reshape(1, n))


def _pack_w_in(w_in):
    o = np.cumsum((0, D, D, D, D, H, H, H * QK_DIM, KV_RANK, ROPE, D, D))
    dn_qkv = w_in[:, o[0]:o[3]]
    dn_z = w_in[:, o[3]:o[4]]
    ab = w_in[:, o[4]:o[6]]
    mq = w_in[:, o[6]:o[7]].reshape(D, H, QK_DIM)
    ckv = w_in[:, o[7]:o[8]]
    kr = w_in[:, o[8]:o[9]]
    g_dn = w_in[:, o[9]:o[10]]
    g_mla = w_in[:, o[10]:o[11]]
    mq = jnp.pad(mq, ((0, 0), (0, 0), (0, QPAD - QK_DIM))).reshape(D, H * QPAD)
    kr = jnp.pad(kr, ((0, 0), (0, LANES - ROPE)))
    ab = jnp.pad(ab, ((0, 0), (0, LANES - 2 * H)))
    w_p = jnp.concatenate([dn_qkv, dn_z, g_dn, g_mla, ckv, kr, ab], axis=1).astype(bf16)
    return w_p, mq.T.astype(bf16)


def _modulated_norm(x, nw, shift, scale):
    r = lax.rsqrt(jnp.mean(x * x, axis=-1, keepdims=True) + EPS)
    return (x * r * nw) * (1.0 + scale) + shift


def _inproj_kernel(x_ref, mod_ref, nw_ref, w_ref, wq_ref, o_ref, qt_ref, h_ref):
    @pl.when(pl.program_id(2) == 0)
    def _():
        h = _modulated_norm(x_ref[0], nw_ref[...], mod_ref[0, 0:1, :], mod_ref[0, 1:2, :])
        h_ref[...] = h.astype(bf16)
        qt_ref[0] = _dot_nt(wq_ref[...], h_ref[...]).astype(qt_ref.dtype)

    o_ref[0] = _dot(h_ref[...], w_ref[...]).astype(o_ref.dtype)


def _inproj(x, mod3, norm_w, w_p, w_qt):
    B, S, _ = x.shape
    return pl.pallas_call(
        _inproj_kernel,
        grid=(B, S // TM_IN, NP // TN_IN),
        in_specs=[pl.BlockSpec((1, TM_IN, D), lambda b, i, j: (b, i, 0)),
                  pl.BlockSpec((1, 6, D), lambda b, i, j: (b, 0, 0)),
                  pl.BlockSpec((1, D), lambda b, i, j: (0, 0)),
                  pl.BlockSpec((D, TN_IN), lambda b, i, j: (0, j)),
                  pl.BlockSpec((H * QPAD, D), lambda b, i, j: (0, 0))],
        out_specs=[pl.BlockSpec((1, TM_IN, TN_IN), lambda b, i, j: (b, i, j)),
                   pl.BlockSpec((1, H * QPAD, TM_IN), lambda b, i, j: (b, 0, i))],
        out_shape=[jax.ShapeDtypeStruct((B, S, NP), bf16),
                   jax.ShapeDtypeStruct((B, H * QPAD, S), bf16)],
        scratch_shapes=[pltpu.VMEM((TM_IN, D), bf16)],
        compiler_params=_cparams(3),
        name="inproj",
    )(x, mod3, norm_w, w_p, w_qt)


def _cumsum_rows(x):
    n = x.shape[0]
    row = lax.broadcasted_iota(jnp.int32, x.shape, 0)
    s = 1
    while s < n:
        x = x + jnp.where(row >= s, pltpu.roll(x, s, 0), 0.0)
        s *= 2
    return x


def _pair_cols(t, ca, cb, rows):
    return jnp.concatenate([jnp.broadcast_to(t[:, ca:ca + 1], (rows, LANES)),
                            jnp.broadcast_to(t[:, cb:cb + 1], (rows, LANES))], axis=1)


def _block_diag_rows(x):
    z = jnp.zeros((x.shape[0], LANES), x.dtype)
    return jnp.concatenate([jnp.concatenate([x[:, :LANES], z], axis=1),
                            jnp.concatenate([z, x[:, LANES:]], axis=1)], axis=0)


def _doubling_masks(n, block):
    r = lax.broadcasted_iota(jnp.int32, (n, n), 0)
    c = lax.broadcasted_iota(jnp.int32, (n, n), 1)
    rx = r ^ c
    eye = rx == 0
    pair_mask = (rx == 1) & ((r & 1) == 1)
    levels = []
    s = 2
    while s < block:
        levels.append((rx >= s) & (rx < 2 * s) & ((r & s) != 0))
        s *= 2
    return eye, pair_mask, levels


def _deltanet_kernel(q_ref, k_ref, v_ref, z_ref, ab_ref, cw_ref, alog_ref, dtb_ref, nw_ref,
                     o_ref, xpad_ref, state_ref):
    C = DN_C
    P2 = 2 * LANES
    c = pl.program_id(1)

    @pl.when(c == 0)
    def _():
        xpad_ref[:, 0:SUBLANES, :] = jnp.zeros((DN_NB, SUBLANES, 3 * D), f32)
        state_ref[...] = jnp.zeros_like(state_ref)

    @pl.when(c > 0)
    def _():
        xpad_ref[:, 0:SUBLANES, :] = xpad_ref[:, C:C + SUBLANES, :]

    xpad_ref[:, SUBLANES:SUBLANES + C, 0:D] = q_ref[...].astype(f32)
    xpad_ref[:, SUBLANES:SUBLANES + C, D:2 * D] = k_ref[...].astype(f32)
    xpad_ref[:, SUBLANES:SUBLANES + C, 2 * D:3 * D] = v_ref[...].astype(f32)

    row = lax.broadcasted_iota(jnp.int32, (C, P2), 0)
    col = lax.broadcasted_iota(jnp.int32, (C, P2), 1) & (LANES - 1)
    tri = row >= col
    strict = row > col
    masks = _doubling_masks(P2, C)
    r2 = lax.broadcasted_iota(jnp.int32, (P2, P2), 0)
    c2 = lax.broadcasted_iota(jnp.int32, (P2, P2), 1)
    same_head = (r2 < LANES) == (c2 < LANES)

    def conv_silu(bi, col0):
        sl = slice(col0, col0 + P2)
        xa = xpad_ref[bi, :, sl]
        y = cw_ref[3:4, sl] * xa[SUBLANES:]
        for j in range(1, DN_CONV):
            y = y + cw_ref[3 - j:4 - j, sl] * pltpu.roll(xa, j, 0)[SUBLANES:]
        return _silu(y)

    def l2n(t):
        t2 = t * t
        ra = lax.rsqrt(jnp.sum(t2[:, :LANES], axis=-1, keepdims=True) + EPS)
        rb = lax.rsqrt(jnp.sum(t2[:, LANES:], axis=-1, keepdims=True) + EPS)
        return jnp.concatenate([t[:, :LANES] * ra, t[:, LANES:] * rb], axis=1)

    n_pairs = H // 2
    chains = [(bi, p) for bi in range(DN_NB) for p in range(n_pairs)]
    decays = []
    for bi in range(DN_NB):
        ab = ab_ref[bi].astype(f32)
        x = ab + dtb_ref[...]
        softplus = jnp.maximum(x, 0.0) + jnp.log1p(jnp.exp(-jnp.abs(x)))
        gc = _cumsum_rows(-jnp.exp(alog_ref[...]) * softplus)
        decays.append((gc, gc.T, _sigmoid(ab)))
    pr = []
    for bi, p in chains:
        gc, gc_t, beta_t = decays[bi]
        ha, hb = 2 * p, 2 * p + 1
        q = l2n(conv_silu(bi, p * P2)) * (DK ** -0.5)
        k = l2n(conv_silu(bi, D + p * P2))
        v = conv_silu(bi, 2 * D + p * P2)
        gcb = _pair_cols(gc, ha, hb, C)
        betab = _pair_cols(beta_t, H + ha, H + hb, C)
        gcr = jnp.concatenate([jnp.broadcast_to(gc_t[ha:ha + 1, :], (C, LANES)),
                               jnp.broadcast_to(gc_t[hb:hb + 1, :], (C, LANES))], axis=1)
        decay = jnp.exp(jnp.where(tri, gcb - gcr, -1e30))
        eg = jnp.exp(gcb)
        gl = jnp.concatenate([jnp.broadcast_to(gc[C - 1:C, ha:ha + 1], (C, LANES)),
                              jnp.broadcast_to(gc[C - 1:C, hb:hb + 1], (C, LANES))], axis=1)
        kb = k * betab
        vb = v * betab
        gram = _dot_nt(jnp.concatenate([kb, q], axis=0).astype(bf16),
                       _block_diag_rows(k).astype(bf16))
        kbe = kb * eg
        rhs = jnp.concatenate([jnp.concatenate([vb[:, :LANES], kbe[:, :LANES]], axis=1),
                               jnp.concatenate([vb[:, LANES:], kbe[:, LANES:]], axis=1)], axis=0).astype(bf16)
        pr.append(dict(
            a_bd=_block_diag_rows(jnp.where(strict, gram[:C] * decay, 0.0)),
            intra=jnp.where(tri, gram[C:] * decay, 0.0).astype(bf16),
            rhs=rhs, qe=q * eg, k_dec=(k * jnp.exp(gl - gcb)).astype(bf16),
            dec_rows=jnp.concatenate(
                [jnp.broadcast_to(jnp.exp(gc[C - 1:C, ha:ha + 1]), (LANES, P2)),
                 jnp.broadcast_to(jnp.exp(gc[C - 1:C, hb:hb + 1]), (LANES, P2))], axis=0)))

    eye, pair_mask, level_masks = masks
    level_sel = [jnp.where(m, 1.0, 0.0).astype(bf16) for m in level_masks]
    a16s = [d["a_bd"].astype(bf16) for d in pr]
    ts = [(jnp.where(eye, 1.0, 0.0) - jnp.where(pair_mask, d["a_bd"], 0.0)).astype(bf16) for d in pr]
    for sel in level_sel:
        inner = [_dot(a16 * sel, t).astype(bf16) for a16, t in zip(a16s, ts)]
        ts = [t - _dot(t, x).astype(bf16) for t, x in zip(ts, inner)]

    sols = [_dot(t, d["rhs"]) for t, d in zip(ts, pr)]
    us = [jnp.concatenate([s[:C, :LANES], s[C:, :LANES]], axis=1) for s in sols]
    ws = [jnp.concatenate([s[:C, LANES:], s[C:, LANES:]], axis=1) for s in sols]
    s2s = [state_ref[n] for n in range(len(chains))]
    wqs = [_dot(jnp.concatenate([w, d["qe"]], axis=0).astype(bf16), s2.astype(bf16))
           for w, d, s2 in zip(ws, pr, s2s)]
    v_news = [u - wq[:C] for u, wq in zip(us, wqs)]
    outs = [wq[C:] + _dot(d["intra"], _block_diag_rows(vn).astype(bf16)) for wq, d, vn in zip(wqs, pr, v_news)]
    kvs = [_dot_tn(d["k_dec"], vn.astype(bf16)) for d, vn in zip(pr, v_news)]
    for n, (bi, p) in enumerate(chains):
        state_ref[n] = s2s[n] * pr[n]["dec_rows"] + jnp.where(same_head, kvs[n], 0.0)
        o = outs[n]
        o2 = o * o
        ra = lax.rsqrt(jnp.mean(o2[:, :LANES], axis=-1, keepdims=True) + EPS)
        rb = lax.rsqrt(jnp.mean(o2[:, LANES:], axis=-1, keepdims=True) + EPS)
        on = jnp.concatenate([o[:, :LANES] * ra * nw_ref[...], o[:, LANES:] * rb * nw_ref[...]], axis=1)
        zz = z_ref[bi, :, p * P2:(p + 1) * P2].astype(f32)
        o_ref[bi, :, p * P2:(p + 1) * P2] = (on * _silu(zz)).astype(o_ref.dtype)


def _deltanet(proj, conv_w, a_log, dt_bias, norm_w):
    B, S, _ = proj.shape
    C = DN_C
    pad = lambda t: jnp.pad(t.reshape(1, H), ((0, 0), (0, LANES - H)))
    col = lambda idx, width: pl.BlockSpec((DN_NB, C, width), lambda b, c: (b, c, idx))
    const = lambda shape: pl.BlockSpec(shape, lambda b, c: (0,) * len(shape))
    return pl.pallas_call(
        _deltanet_kernel,
        grid=(B // DN_NB, S // C),
        in_specs=[col(0, D), col(1, D), col(2, D), col(P_Z // D, D), col(P_AB // LANES, LANES),
                  const((DN_CONV, 3 * D)), const((1, LANES)), const((1, LANES)), const((1, DK))],
        out_specs=pl.BlockSpec((DN_NB, C, D), lambda b, c: (b, c, 0)),
        out_shape=jax.ShapeDtypeStruct((B, S, D), bf16),
        scratch_shapes=[pltpu.VMEM((DN_NB, C + SUBLANES, 3 * D), f32),
                        pltpu.VMEM((DN_NB * (H // 2), 2 * LANES, 2 * LANES), f32)],
        compiler_params=_cparams(2),
        name="deltanet",
    )(proj, proj, proj, proj, proj, conv_w, pad(a_log), pad(dt_bias), norm_w.reshape(1, DK))


def _rope(r, cos, sin_lo, sin_hi):
    return r * cos + pltpu.roll(r, LANES - ROPE // 2, 1) * sin_lo + pltpu.roll(r, ROPE // 2, 1) * sin_hi


def _mlaprep_kernel(qt_ref, ckv_ref, kr_ref, posc_ref, posr_ref, invr_ref, invc_ref, kvw_ref, qw_ref,
                    kwn_ref, kwr_ref, wuk_ref, wuvt_ref, qo_ref, ko_ref, vo_ref):
    tm = ckv_ref.shape[1]
    half = ROPE // 2

    ang = posc_ref[0].astype(f32) * invr_ref[...]
    lane = lax.broadcasted_iota(jnp.int32, (tm, LANES), 1)
    cos = jnp.cos(ang)
    sin = jnp.sin(ang)
    sin_lo = jnp.where(lane < half, -sin, 0.0)
    sin_hi = jnp.where((lane >= half) & (lane < ROPE), sin, 0.0)

    ckv = ckv_ref[0].astype(f32)
    ckv_n = (ckv * lax.rsqrt(jnp.mean(ckv * ckv, axis=-1, keepdims=True) + EPS) * kvw_ref[...]).astype(bf16)
    k_nope = _dot(ckv_n, wuk_ref[...])
    vo_ref[0] = _dot_nt(wuvt_ref[...], ckv_n).astype(vo_ref.dtype)

    kr = kr_ref[0].astype(f32)
    kr_ss = jnp.sum(kr * kr, axis=-1, keepdims=True)
    kr_rot = _rope(kr * kwr_ref[...], cos, sin_lo, sin_hi)
    for h in range(H):
        kn = k_nope[:, h * NOPE:(h + 1) * NOPE]
        rk = lax.rsqrt((jnp.sum(kn * kn, axis=-1, keepdims=True) + kr_ss) * (1.0 / QK_DIM) + EPS)
        ko_ref[0, :, h * QPAD:h * QPAD + NOPE] = (kn * rk * kwn_ref[...]).astype(ko_ref.dtype)
        ko_ref[0, :, h * QPAD + NOPE:(h + 1) * QPAD] = (kr_rot * rk).astype(ko_ref.dtype)

    ang_t = _lane_tile(invc_ref[...], tm) * posr_ref[0].astype(f32)
    cos_t = jnp.cos(ang_t)
    sin_t = jnp.sin(ang_t)
    qw_t = _lane_tile(qw_ref[...], tm)
    for h in range(H):
        qh = qt_ref[0, h * QPAD:(h + 1) * QPAD, :].astype(f32)
        rq = lax.rsqrt(jnp.sum(qh * qh, axis=0, keepdims=True) * (1.0 / QK_DIM) + EPS) * (QK_DIM ** -0.5 * LOG2E)
        qn = qh * rq * qw_t
        x1 = qn[NOPE:NOPE + half]
        x2 = qn[NOPE + half:QK_DIM]
        r0 = h * QPAD
        qo_ref[0, r0:r0 + NOPE, :] = qn[:NOPE].astype(qo_ref.dtype)
        qo_ref[0, r0 + NOPE:r0 + NOPE + half, :] = (x1 * cos_t - x2 * sin_t).astype(qo_ref.dtype)
        qo_ref[0, r0 + NOPE + half:r0 + QK_DIM, :] = (x2 * cos_t + x1 * sin_t).astype(qo_ref.dtype)
        qo_ref[0, r0 + QK_DIM:r0 + QPAD, :] = jnp.zeros((QPAD - QK_DIM, tm), qo_ref.dtype)


def _mlaprep(proj, q_t, positions, kv_norm_w, w_uk, w_uv, q_norm_w, k_norm_w):
    B, S, _ = proj.shape
    tm = TM_PREP
    half = ROPE // 2
    inv = ROPE_THETA ** (-jnp.arange(half, dtype=f32) / half)
    inv_r = jnp.concatenate([inv, inv, jnp.zeros((LANES - ROPE,), f32)]).reshape(1, LANES)
    inv_c = jnp.broadcast_to(inv.reshape(half, 1), (half, LANES))
    qw = jnp.broadcast_to(jnp.pad(q_norm_w, (0, QPAD - QK_DIM)).reshape(QPAD, 1), (QPAD, LANES))
    kwn = k_norm_w[:NOPE].reshape(1, NOPE)
    kwr = jnp.pad(k_norm_w[NOPE:], (0, LANES - ROPE)).reshape(1, LANES)
    col = lambda idx, width: pl.BlockSpec((1, tm, width), lambda b, i: (b, i, idx))
    const = lambda shape: pl.BlockSpec(shape, lambda b, i: (0,) * len(shape))
    return pl.pallas_call(
        _mlaprep_kernel,
        grid=(B, S // tm),
        in_specs=[pl.BlockSpec((1, H * QPAD, tm), lambda b, i: (b, 0, i)),
                  col(P_CKV // KV_RANK, KV_RANK), col(P_KR // LANES, LANES),
                  pl.BlockSpec((1, tm, 1), lambda b, i: (b, i, 0)),
                  pl.BlockSpec((1, 1, tm), lambda b, i: (b, 0, i)),
                  const((1, LANES)), const((half, LANES)), const((1, KV_RANK)), const((QPAD, LANES)),
                  const((1, NOPE)), const((1, LANES)),
                  const((KV_RANK, H * NOPE)), const((H * DV, KV_RANK))],
        out_specs=[pl.BlockSpec((1, H * QPAD, tm), lambda b, i: (b, 0, i)),
                   pl.BlockSpec((1, tm, H * QPAD), lambda b, i: (b, i, 0)),
                   pl.BlockSpec((1, H * DV, tm), lambda b, i: (b, 0, i))],
        out_shape=[jax.ShapeDtypeStruct((B, H * QPAD, S), bf16),
                   jax.ShapeDtypeStruct((B, S, H * QPAD), bf16),
                   jax.ShapeDtypeStruct((B, H * DV, S), bf16)],
        compiler_params=_cparams(2),
        name="mlaprep",
    )(q_t, proj, proj, positions.reshape(B, S, 1), positions.reshape(B, 1, S), inv_r, inv_c,
      kv_norm_w.reshape(1, KV_RANK), qw, kwn, kwr, w_uk.astype(bf16), w_uv.T.astype(bf16))


def _attn_kernel(q_ref, k_ref, v_ref, o_ref, m_ref, acc_ref, s_ref):
    G = ATT_G
    i = pl.program_id(2)
    m_ref[...] = jnp.full_like(m_ref, -jnp.inf)
    acc_ref[...] = jnp.zeros_like(acc_ref)
    last = ((i + 1) * TQ - 1) // TK

    def qk(j, slot):
        start = pl.multiple_of(j * TK, TK)
        for g in range(G):
            s_ref[slot, g] = _dot(k_ref[0, pl.ds(start, TK), g * QPAD:(g + 1) * QPAD],
                                  q_ref[0, g * QPAD:(g + 1) * QPAD, :])

    def softmax_pv(j, slot, masked):
        start = pl.multiple_of(j * TK, TK)
        if masked:
            kc = (lax.broadcasted_iota(jnp.int32, (TK, TQ), 0) + j * TK) // CHUNK
            qc = (lax.broadcasted_iota(jnp.int32, (TK, TQ), 1) + i * TQ) // CHUNK
            visible = qc >= kc
        probs, alphas = [], []
        for g in range(G):
            s = s_ref[slot, g]
            if masked:
                s = jnp.where(visible, s, -jnp.inf)
            m_prev = m_ref[g]
            m_new = jnp.maximum(m_prev, jnp.max(s, axis=0, keepdims=True))
            alphas.append(jnp.exp2(m_prev - m_new))
            probs.append(jnp.exp2(s - m_new).astype(bf16))
            m_ref[g] = m_new
        ones = jnp.ones((DV, TK), bf16)
        for g in range(G):
            v1 = jnp.concatenate([v_ref[0, g * DV:(g + 1) * DV, pl.ds(start, TK)], ones], axis=0)
            acc_ref[g] = alphas[g] * acc_ref[g] + _dot(v1, probs[g])

    qk(0, 0)

    def body(p, carry):
        qk(2 * p + 1, 1)
        softmax_pv(2 * p, 0, False)
        qk(2 * p + 2, 0)
        softmax_pv(2 * p + 1, 1, False)
        return carry

    lax.fori_loop(0, last // 2, body, 0)

    @pl.when(last % 2 == 0)
    def _():
        softmax_pv(last, 0, True)

    @pl.when(last % 2 == 1)
    def _():
        qk(last, 1)
        softmax_pv(last - 1, 0, False)
        softmax_pv(last, 1, True)

    for g in range(G):
        acc = acc_ref[g]
        o_ref[0, :, g * DV:(g + 1) * DV] = (acc[:DV] / acc[DV:]).T.astype(o_ref.dtype)


def _attn(q_t, kp, v_t):
    B, S, _ = kp.shape
    G = ATT_G
    return pl.pallas_call(
        _attn_kernel,
        grid=(B, H // G, S // TQ),
        in_specs=[pl.BlockSpec((1, G * QPAD, TQ), lambda b, h, i: (b, h, i)),
                  pl.BlockSpec((1, S, G * QPAD), lambda b, h, i: (b, 0, h)),
                  pl.BlockSpec((1, G * DV, S), lambda b, h, i: (b, h, 0))],
        out_specs=pl.BlockSpec((1, TQ, G * DV), lambda b, h, i: (b, i, h)),
        out_shape=jax.ShapeDtypeStruct((B, S, H * DV), bf16),
        scratch_shapes=[pltpu.VMEM((G, 1, TQ), f32), pltpu.VMEM((G, 2 * DV, TQ), f32),
                        pltpu.VMEM((2, G, TK, TQ), f32)],
        compiler_params=_cparams(3),
        name="attn",
    )(q_t, kp, v_t)


def _merge_kernel(x_ref, oa_ref, ob_ref, gd_ref, gm_ref, mod_ref, wa_ref, wb_ref, wo_ref, o_ref):
    y_a = _dot(oa_ref[0], wa_ref[...])
    y_b = _dot(ob_ref[0], wb_ref[...])
    mix = _sigmoid(gd_ref[0].astype(f32)) * y_a + _sigmoid(gm_ref[0].astype(f32)) * y_b
    o_ref[0] = x_ref[0] + mod_ref[0, 2:3, :] * _dot(mix.astype(bf16), wo_ref[...])


def _merge(x, o_a, o_b, proj, mod3, w_a, w_b, w_o):
    B, S, _ = x.shape
    tm = TM_MERGE
    row = lambda idx: pl.BlockSpec((1, tm, D), lambda b, i: (b, i, idx))
    wspec = pl.BlockSpec((D, D), lambda b, i: (0, 0))
    return pl.pallas_call(
        _merge_kernel,
        grid=(B, S // tm),
        in_specs=[row(0), row(0), row(0), row(P_GDN // D), row(P_GMLA // D),
                  pl.BlockSpec((1, 6, D), lambda b, i: (b, 0, 0)), wspec, wspec, wspec],
        out_specs=row(0),
        out_shape=jax.ShapeDtypeStruct((B, S, D), f32),
        compiler_params=_cparams(2),
        name="merge",
    )(x, o_a, o_b, proj, proj, mod3, w_a.astype(bf16), w_b.astype(bf16), w_o.astype(bf16))


def _ffn_kernel(x_ref, halo_ref, mod_ref, nw_ref, wa_ref, wv_ref, cw_ref, cb_ref, wd_ref, o_ref,
                h_ref, a_ref, acc_ref):
    tm = x_ref.shape[1]
    i = pl.program_id(1)
    j = pl.program_id(2)

    @pl.when(j == 0)
    def _():
        shift, scale = mod_ref[0, 3:4, :], mod_ref[0, 4:5, :]
        hh = _modulated_norm(halo_ref[0], nw_ref[...], shift, scale)
        h_ref[0:SUBLANES, :] = jnp.where(i > 0, hh, 0.0).astype(bf16)
        h_ref[SUBLANES:, :] = _modulated_norm(x_ref[0], nw_ref[...], shift, scale).astype(bf16)
        acc_ref[...] = jnp.zeros_like(acc_ref)

    a_ref[...] = _dot(h_ref[...], wa_ref[...])
    v = _dot(h_ref[SUBLANES:, :], wv_ref[...])
    a = cb_ref[...] + cw_ref[2:3, :] * a_ref[8:8 + tm, :]
    for t in range(1, FFN_CONV):
        a = a + cw_ref[2 - t:3 - t, :] * a_ref[8 - t:8 - t + tm, :]
    gelu = 0.5 * a * (1.0 + lax.erf(a * (2.0 ** -0.5)))
    acc_ref[...] += _dot((gelu * v).astype(bf16), wd_ref[...])

    @pl.when(j == pl.num_programs(2) - 1)
    def _():
        o_ref[0] = x_ref[0] + mod_ref[0, 5:6, :] * acc_ref[...]


def _ffn(x, mod3, norm_w, w_up, conv_w, conv_b, w_down):
    B, S, _ = x.shape
    tm, tf = TM_FFN, TF_FFN
    nf = D_FF // tf
    hb = tm // SUBLANES
    return pl.pallas_call(
        _ffn_kernel,
        grid=(B, S // tm, nf),
        in_specs=[pl.BlockSpec((1, tm, D), lambda b, i, j: (b, i, 0)),
                  pl.BlockSpec((1, SUBLANES, D), lambda b, i, j: (b, jnp.maximum(i * hb - 1, 0), 0)),
                  pl.BlockSpec((1, 6, D), lambda b, i, j: (b, 0, 0)),
                  pl.BlockSpec((1, D), lambda b, i, j: (0, 0)),
                  pl.BlockSpec((D, tf), lambda b, i, j: (0, j)),
                  pl.BlockSpec((D, tf), lambda b, i, j: (0, nf + j)),
                  pl.BlockSpec((FFN_CONV, tf), lambda b, i, j: (0, j)),
                  pl.BlockSpec((1, tf), lambda b, i, j: (0, j)),
                  pl.BlockSpec((tf, D), lambda b, i, j: (j, 0))],
        out_specs=pl.BlockSpec((1, tm, D), lambda b, i, j: (b, i, 0)),
        out_shape=jax.ShapeDtypeStruct((B, S, D), f32),
        scratch_shapes=[pltpu.VMEM((tm + SUBLANES, D), bf16),
                        pltpu.VMEM((tm + SUBLANES, tf), f32),
                        pltpu.VMEM((tm, D), f32)],
        compiler_params=_cparams(3),
        name="ffn",
    )(x, x, mod3, norm_w, w_up, w_up, conv_w, conv_b.reshape(1, D_FF), w_down)


def kernel(x, c, positions, w_ada, b_ada, norm1_w, w_in, dn_conv_w, dn_a_log, dn_dt_bias, dn_norm_w,
           mla_kv_norm_w, mla_w_uk, mla_w_uv, mla_q_norm_w, mla_k_norm_w, w_out_dn, w_out_mla, w_o,
           norm2_w, ffn_w_up, ffn_conv_w, ffn_conv_b, ffn_w_down):
    B = x.shape[0]
    for l in range(w_ada.shape[0]):
        mod3 = _mod(c, w_ada[l], b_ada[l]).reshape(B, 6, D)
        w_p, w_qt = _pack_w_in(w_in[l])
        proj, q_t = _inproj(x, mod3, norm1_w[l].reshape(1, D), w_p, w_qt)
        o_a = _deltanet(proj, dn_conv_w[l], dn_a_log[l], dn_dt_bias[l], dn_norm_w[l])
        qp_t, kp, v_t = _mlaprep(proj, q_t, positions, mla_kv_norm_w[l], mla_w_uk[l], mla_w_uv[l],
                                 mla_q_norm_w[l], mla_k_norm_w[l])
        o_b = _attn(qp_t, kp, v_t)
        x = _merge(x, o_a, o_b, proj, mod3, w_out_dn[l], w_out_mla[l], w_o[l])
        x = _ffn(x, mod3, norm2_w[l].reshape(1, D), ffn_w_up[l].astype(bf16), ffn_conv_w[l],
                 ffn_conv_b[l], ffn_w_down[l].astype(bf16))
    return x
```

```python
import jax
import jax.numpy as jnp
import numpy as np
from jax import lax
from jax.experimental import pallas as pl
from jax.experimental.pallas import tpu as pltpu

f32 = jnp.float32
bf16 = jnp.bfloat16

D = 1024
EPS = 1e-6
CHUNK = 64
H = 8
DK = 128
NOPE, ROPE, DV = 128, 64, 128
QK_DIM = NOPE + ROPE
KV_RANK = 256
ROPE_THETA = 10000.0
D_FF = 2816
DN_CONV = 4
FFN_CONV = 3
LOG2E = 1.4426950408889634

LANES = 128
SUBLANES = 8
VMEM_LIMIT = 56 * 1024 * 1024

QPAD = 2 * LANES
P_DNQKV = 0
P_Z = 3 * D
P_GDN = 4 * D
P_GMLA = 5 * D
P_CKV = 6 * D
P_KR = P_CKV + KV_RANK
P_AB = P_KR + LANES
NP = P_AB + LANES

DN_C = 128
DN_NB = 2
TM_IN, TN_IN = 1024, 1664
TM_PREP = 512
TQ = 512
TK = 512
ATT_G = 4
TM_MERGE = 512
TM_FFN, TF_FFN = 1024, 1408


def _cparams(n_axes):
    return pltpu.CompilerParams(dimension_semantics=("arbitrary",) * n_axes,
                                vmem_limit_bytes=VMEM_LIMIT)


def _sigmoid(x):
    return 0.5 * (1.0 + jnp.tanh(0.5 * x))


def _silu(x):
    return x * _sigmoid(x)


def _dot(a, b):
    return jnp.dot(a, b, preferred_element_type=f32)


def _dot_nt(a, b):
    return lax.dot_general(a, b, (((1,), (1,)), ((), ())), preferred_element_type=f32)


def _dot_tn(a, b):
    return lax.dot_general(a, b, (((0,), (0,)), ((), ())), preferred_element_type=f32)


def _lane_tile(t, width):
    return jnp.concatenate([t] * (width // LANES), axis=1)


def _mod_kernel(c_ref, w_ref, b_ref, o_ref):
    a = _silu(c_ref[...]).astype(bf16)
    o_ref[...] = _dot(a, w_ref[...].astype(bf16)) + b_ref[...]


def _mod(c, w_ada, b_ada):
    B = c.shape[0]
    n = w_ada.shape[1]
    return pl.pallas_call(
        _mod_kernel,
        grid=(n // D,),
        in_specs=[pl.BlockSpec((B, D), lambda j: (0, 0)),
                  pl.BlockSpec((D, D), lambda j: (0, j)),
                  pl.BlockSpec((1, D), lambda j: (0, j))],
        out_specs=pl.BlockSpec((B, D), lambda j: (0, j)),
        out_shape=jax.ShapeDtypeStruct((B, n), f32),
        compiler_params=_cparams(1),
        name="mod",
    )(c, w_ada, b_ada.reshape(1, n))


def _pack_w_in(w_in):
    o = np.cumsum((0, D, D, D, D, H, H, H * QK_DIM, KV_RANK, ROPE, D, D))
    dn_qkv = w_in[:, o[0]:o[3]]
    dn_z = w_in[:, o[3]:o[4]]
    ab = w_in[:, o[4]:o[6]]
    mq = w_in[:, o[6]:o[7]].reshape(D, H, QK_DIM)
    ckv = w_in[:, o[7]:o[8]]
    kr = w_in[:, o[8]:o[9]]
    g_dn = w_in[:, o[9]:o[10]]
    g_mla = w_in[:, o[10]:o[11]]
    mq = jnp.pad(mq, ((0, 0), (0, 0), (0, QPAD - QK_DIM))).reshape(D, H * QPAD)
    kr = jnp.pad(kr, ((0, 0), (0, LANES - ROPE)))
    ab = jnp.pad(ab, ((0, 0), (0, LANES - 2 * H)))
    w_p = jnp.concatenate([dn_qkv, dn_z, g_dn, g_mla, ckv, kr, ab], axis=1).astype(bf16)
    return w_p, mq.T.astype(bf16)


def _modulated_norm(x, nw, shift, scale):
    r = lax.rsqrt(jnp.mean(x * x, axis=-1, keepdims=True) + EPS)
    return (x * r * nw) * (1.0 + scale) + shift


def _inproj_kernel(x_ref, mod_ref, nw_ref, w_ref, wq_ref, o_ref, qt_ref, h_ref):
    @pl.when(pl.program_id(2) == 0)
    def _():
        h = _modulated_norm(x_ref[0], nw_ref[...], mod_ref[0, 0:1, :], mod_ref[0, 1:2, :])
        h_ref[...] = h.astype(bf16)
        qt_ref[0] = _dot_nt(wq_ref[...], h_ref[...]).astype(qt_ref.dtype)

    o_ref[0] = _dot(h_ref[...], w_ref[...]).astype(o_ref.dtype)


def _inproj(x, mod3, norm_w, w_p, w_qt):
    B, S, _ = x.shape
    return pl.pallas_call(
        _inproj_kernel,
        grid=(B, S // TM_IN, NP // TN_IN),
        in_specs=[pl.BlockSpec((1, TM_IN, D), lambda b, i, j: (b, i, 0)),
                  pl.BlockSpec((1, 6, D), lambda b, i, j: (b, 0, 0)),
                  pl.BlockSpec((1, D), lambda b, i, j: (0, 0)),
                  pl.BlockSpec((D, TN_IN), lambda b, i, j: (0, j)),
                  pl.BlockSpec((H * QPAD, D), lambda b, i, j: (0, 0))],
        out_specs=[pl.BlockSpec((1, TM_IN, TN_IN), lambda b, i, j: (b, i, j)),
                   pl.BlockSpec((1, H * QPAD, TM_IN), lambda b, i, j: (b, 0, i))],
        out_shape=[jax.ShapeDtypeStruct((B, S, NP), bf16),
                   jax.ShapeDtypeStruct((B, H * QPAD, S), bf16)],
        scratch_shapes=[pltpu.VMEM((TM_IN, D), bf16)],
        compiler_params=_cparams(3),
        name="inproj",
    )(x, mod3, norm_w, w_p, w_qt)


def _cumsum_rows(x):
    n = x.shape[0]
    row = lax.broadcasted_iota(jnp.int32, x.shape, 0)
    s = 1
    while s < n:
        x = x + jnp.where(row >= s, pltpu.roll(x, s, 0), 0.0)
        s *= 2
    return x


def _pair_cols(t, ca, cb, rows):
    return jnp.concatenate([jnp.broadcast_to(t[:, ca:ca + 1], (rows, LANES)),
                            jnp.broadcast_to(t[:, cb:cb + 1], (rows, LANES))], axis=1)


def _block_diag_rows(x):
    z = jnp.zeros((x.shape[0], LANES), x.dtype)
    return jnp.concatenate([jnp.concatenate([x[:, :LANES], z], axis=1),
                            jnp.concatenate([z, x[:, LANES:]], axis=1)], axis=0)


def _doubling_masks(n, block):
    r = lax.broadcasted_iota(jnp.int32, (n, n), 0)
    c = lax.broadcasted_iota(jnp.int32, (n, n), 1)
    rx = r ^ c
    eye = rx == 0
    pair_mask = (rx == 1) & ((r & 1) == 1)
    levels = []
    s = 2
    while s < block:
        levels.append((rx >= s) & (rx < 2 * s) & ((r & s) != 0))
        s *= 2
    return eye, pair_mask, levels


def _deltanet_kernel(q_ref, k_ref, v_ref, z_ref, ab_ref, cw_ref, alog_ref, dtb_ref, nw_ref,
                     o_ref, xpad_ref, state_ref):
    C = DN_C
    P2 = 2 * LANES
    c = pl.program_id(1)

    @pl.when(c == 0)
    def _():
        xpad_ref[:, C:2 * C, :] = jnp.zeros((DN_NB, C, 3 * D), xpad_ref.dtype)
        state_ref[...] = jnp.zeros_like(state_ref)

    xpad_ref[:, 0:C, :] = xpad_ref[:, C:2 * C, :]
    xpad_ref[:, C:2 * C, 0:D] = q_ref[...]
    xpad_ref[:, C:2 * C, D:2 * D] = k_ref[...]
    xpad_ref[:, C:2 * C, 2 * D:3 * D] = v_ref[...]

    sr = lax.broadcasted_iota(jnp.int32, ((DN_CONV - 1) * C, 2 * C), 0)
    sc = lax.broadcasted_iota(jnp.int32, ((DN_CONV - 1) * C, 2 * C), 1)
    shift = jnp.where(sc == C + sr % C - (sr // C + 1), 1.0, 0.0).astype(bf16)

    row = lax.broadcasted_iota(jnp.int32, (C, P2), 0)
    col = lax.broadcasted_iota(jnp.int32, (C, P2), 1) & (LANES - 1)
    tri = row >= col
    strict = row > col
    masks = _doubling_masks(P2, C)
    r2 = lax.broadcasted_iota(jnp.int32, (P2, P2), 0)
    c2 = lax.broadcasted_iota(jnp.int32, (P2, P2), 1)
    same_head = (r2 < LANES) == (c2 < LANES)

    shifted_all = {(bi, col0): _dot(shift, xpad_ref[bi, :, col0:col0 + P2])
                   for bi in range(DN_NB) for col0 in range(0, 3 * D, P2)}

    def conv_silu(bi, col0):
        sl = slice(col0, col0 + P2)
        xa = xpad_ref[bi, :, sl]
        shifted = shifted_all[bi, col0]
        y = cw_ref[3:4, sl] * xa[C:].astype(f32)
        for j in range(1, DN_CONV):
            y = y + cw_ref[3 - j:4 - j, sl] * shifted[(j - 1) * C:j * C]
        return _silu(y)

    def l2n(t):
        t2 = t * t
        ra = lax.rsqrt(jnp.sum(t2[:, :LANES], axis=-1, keepdims=True) + EPS)
        rb = lax.rsqrt(jnp.sum(t2[:, LANES:], axis=-1, keepdims=True) + EPS)
        return jnp.concatenate([t[:, :LANES] * ra, t[:, LANES:] * rb], axis=1)

    n_pairs = H // 2
    chains = [(bi, p) for bi in range(DN_NB) for p in range(n_pairs)]
    decays = []
    for bi in range(DN_NB):
        ab = ab_ref[bi].astype(f32)
        x = ab + dtb_ref[...]
        softplus = jnp.maximum(x, 0.0) + jnp.log1p(jnp.exp(-jnp.abs(x)))
        gc = _cumsum_rows(-jnp.exp(alog_ref[...]) * softplus)
        decays.append((gc, gc.T, _sigmoid(ab)))
    pr = []
    for bi, p in chains:
        gc, gc_t, beta_t = decays[bi]
        ha, hb = 2 * p, 2 * p + 1
        q = l2n(conv_silu(bi, p * P2)) * (DK ** -0.5)
        k = l2n(conv_silu(bi, D + p * P2))
        v = conv_silu(bi, 2 * D + p * P2)
        gcb = _pair_cols(gc, ha, hb, C)
        betab = _pair_cols(beta_t, H + ha, H + hb, C)
        gcr = jnp.concatenate([jnp.broadcast_to(gc_t[ha:ha + 1, :], (C, LANES)),
                               jnp.broadcast_to(gc_t[hb:hb + 1, :], (C, LANES))], axis=1)
        decay = jnp.exp(jnp.where(tri, gcb - gcr, -1e30))
        eg = jnp.exp(gcb)
        gl = jnp.concatenate([jnp.broadcast_to(gc[C - 1:C, ha:ha + 1], (C, LANES)),
                              jnp.broadcast_to(gc[C - 1:C, hb:hb + 1], (C, LANES))], axis=1)
        kb = k * betab
        vb = v * betab
        gram = _dot_nt(jnp.concatenate([kb, q], axis=0).astype(bf16),
                       _block_diag_rows(k).astype(bf16))
        kbe = kb * eg
        rhs = jnp.concatenate([jnp.concatenate([vb[:, :LANES], kbe[:, :LANES]], axis=1),
                               jnp.concatenate([vb[:, LANES:], kbe[:, LANES:]], axis=1)], axis=0).astype(bf16)
        pr.append(dict(
            a_bd=_block_diag_rows(jnp.where(strict, gram[:C] * decay, 0.0)),
            intra=jnp.where(tri, gram[C:] * decay, 0.0).astype(bf16),
            rhs=rhs, qe=q * eg, k_dec=(k * jnp.exp(gl - gcb)).astype(bf16),
            dec_rows=jnp.concatenate(
                [jnp.broadcast_to(jnp.exp(gc[C - 1:C, ha:ha + 1]), (LANES, P2)),
                 jnp.broadcast_to(jnp.exp(gc[C - 1:C, hb:hb + 1]), (LANES, P2))], axis=0)))

    eye, pair_mask, level_masks = masks
    level_sel = [jnp.where(m, 1.0, 0.0).astype(bf16) for m in level_masks]
    a16s = [d["a_bd"].astype(bf16) for d in pr]
    ts = [(jnp.where(eye, 1.0, 0.0) - jnp.where(pair_mask, d["a_bd"], 0.0)).astype(bf16) for d in pr]
    for sel in level_sel:
        inner = [_dot(a16 * sel, t).astype(bf16) for a16, t in zip(a16s, ts)]
        ts = [t - _dot(t, x).astype(bf16) for t, x in zip(ts, inner)]

    sols = [_dot(t, d["rhs"]) for t, d in zip(ts, pr)]
    us = [jnp.concatenate([s[:C, :LANES], s[C:, :LANES]], axis=1) for s in sols]
    ws = [jnp.concatenate([s[:C, LANES:], s[C:, LANES:]], axis=1) for s in sols]
    s2s = [state_ref[n] for n in range(len(chains))]
    wqs = [_dot(jnp.concatenate([w, d["qe"]], axis=0).astype(bf16), s2.astype(bf16))
           for w, d, s2 in zip(ws, pr, s2s)]
    v_news = [u - wq[:C] for u, wq in zip(us, wqs)]
    outs = [wq[C:] + _dot(d["intra"], _block_diag_rows(vn).astype(bf16)) for wq, d, vn in zip(wqs, pr, v_news)]
    kvs = [_dot_tn(d["k_dec"], vn.astype(bf16)) for d, vn in zip(pr, v_news)]
    for n, (bi, p) in enumerate(chains):
        state_ref[n] = s2s[n] * pr[n]["dec_rows"] + jnp.where(same_head, kvs[n], 0.0)
        o = outs[n]
        o2 = o * o
        ra = lax.rsqrt(jnp.mean(o2[:, :LANES], axis=-1, keepdims=True) + EPS)
        rb = lax.rsqrt(jnp.mean(o2[:, LANES:], axis=-1, keepdims=True) + EPS)
        on = jnp.concatenate([o[:, :LANES] * ra * nw_ref[...], o[:, LANES:] * rb * nw_ref[...]], axis=1)
        zz = z_ref[bi, :, p * P2:(p + 1) * P2].astype(f32)
        o_ref[bi, :, p * P2:(p + 1) * P2] = (on * _silu(zz)).astype(o_ref.dtype)


def _deltanet(proj, conv_w, a_log, dt_bias, norm_w):
    B, S, _ = proj.shape
    C = DN_C
    pad = lambda t: jnp.pad(t.reshape(1, H), ((0, 0), (0, LANES - H)))
    col = lambda idx, width: pl.BlockSpec((DN_NB, C, width), lambda b, c: (b, c, idx))
    const = lambda shape: pl.BlockSpec(shape, lambda b, c: (0,) * len(shape))
    return pl.pallas_call(
        _deltanet_kernel,
        grid=(B // DN_NB, S // C),
        in_specs=[col(0, D), col(1, D), col(2, D), col(P_Z // D, D), col(P_AB // LANES, LANES),
                  const((DN_CONV, 3 * D)), const((1, LANES)), const((1, LANES)), const((1, DK))],
        out_specs=pl.BlockSpec((DN_NB, C, D), lambda b, c: (b, c, 0)),
        out_shape=jax.ShapeDtypeStruct((B, S, D), bf16),
        scratch_shapes=[pltpu.VMEM((DN_NB, 2 * C, 3 * D), proj.dtype),
                        pltpu.VMEM((DN_NB * (H // 2), 2 * LANES, 2 * LANES), f32)],
        compiler_params=_cparams(2),
        name="deltanet",
    )(proj, proj, proj, proj, proj, conv_w, pad(a_log), pad(dt_bias), norm_w.reshape(1, DK))


def _rope(r, cos, sin_lo, sin_hi):
    return r * cos + pltpu.roll(r, LANES - ROPE // 2, 1) * sin_lo + pltpu.roll(r, ROPE // 2, 1) * sin_hi


def _mlaprep_kernel(qt_ref, ckv_ref, kr_ref, pos_ref, inv_ref, kvw_ref, qw_ref,
                    kwn_ref, kwr_ref, wuk_ref, wuvt_ref, qo_ref, ko_ref, vo_ref):
    tm = ckv_ref.shape[1]
    half = ROPE // 2

    ang_t = _lane_tile(inv_ref[...], tm) * pos_ref[0].astype(f32)
    cos_t = jnp.cos(ang_t)
    sin_t = jnp.sin(ang_t)
    z_half = jnp.zeros((half, tm), f32)
    z_pad = jnp.zeros((LANES - ROPE, tm), f32)
    cos = jnp.concatenate([cos_t, cos_t, z_pad], axis=0).T
    sin_lo = jnp.concatenate([-sin_t, z_half, z_pad], axis=0).T
    sin_hi = jnp.concatenate([z_half, sin_t, z_pad], axis=0).T

    ckv = ckv_ref[0].astype(f32)
    ckv_n = (ckv * lax.rsqrt(jnp.mean(ckv * ckv, axis=-1, keepdims=True) + EPS) * kvw_ref[...]).astype(bf16)
    k_nope = _dot(ckv_n, wuk_ref[...])
    vo_ref[0] = _dot_nt(wuvt_ref[...], ckv_n).astype(vo_ref.dtype)

    kr = kr_ref[0].astype(f32)
    kr_ss = jnp.sum(kr * kr, axis=-1, keepdims=True)
    kr_rot = _rope(kr * kwr_ref[...], cos, sin_lo, sin_hi)
    for h in range(H):
        kn = k_nope[:, h * NOPE:(h + 1) * NOPE]
        rk = lax.rsqrt((jnp.sum(kn * kn, axis=-1, keepdims=True) + kr_ss) * (1.0 / QK_DIM) + EPS)
        ko_ref[0, :, h * QPAD:h * QPAD + NOPE] = (kn * rk * kwn_ref[...]).astype(ko_ref.dtype)
        ko_ref[0, :, h * QPAD + NOPE:(h + 1) * QPAD] = (kr_rot * rk).astype(ko_ref.dtype)

    qw_t = _lane_tile(qw_ref[...], tm)
    for h in range(H):
        qh = qt_ref[0, h * QPAD:(h + 1) * QPAD, :].astype(f32)
        rq = lax.rsqrt(jnp.sum(qh * qh, axis=0, keepdims=True) * (1.0 / QK_DIM) + EPS) * (QK_DIM ** -0.5 * LOG2E)
        qn = qh * rq * qw_t
        x1 = qn[NOPE:NOPE + half]
        x2 = qn[NOPE + half:QK_DIM]
        r0 = h * QPAD
        qo_ref[0, r0:r0 + NOPE, :] = qn[:NOPE].astype(qo_ref.dtype)
        qo_ref[0, r0 + NOPE:r0 + NOPE + half, :] = (x1 * cos_t - x2 * sin_t).astype(qo_ref.dtype)
        qo_ref[0, r0 + NOPE + half:r0 + QK_DIM, :] = (x2 * cos_t + x1 * sin_t).astype(qo_ref.dtype)
        qo_ref[0, r0 + QK_DIM:r0 + QPAD, :] = jnp.zeros((QPAD - QK_DIM, tm), qo_ref.dtype)


def _mlaprep(proj, q_t, positions, kv_norm_w, w_uk, w_uv, q_norm_w, k_norm_w):
    B, S, _ = proj.shape
    tm = TM_PREP
    half = ROPE // 2
    inv = ROPE_THETA ** (-jnp.arange(half, dtype=f32) / half)
    inv_c = jnp.broadcast_to(inv.reshape(half, 1), (half, LANES))
    qw = jnp.broadcast_to(jnp.pad(q_norm_w, (0, QPAD - QK_DIM)).reshape(QPAD, 1), (QPAD, LANES))
    kwn = k_norm_w[:NOPE].reshape(1, NOPE)
    kwr = jnp.pad(k_norm_w[NOPE:], (0, LANES - ROPE)).reshape(1, LANES)
    col = lambda idx, width: pl.BlockSpec((1, tm, width), lambda b, i: (b, i, idx))
    const = lambda shape: pl.BlockSpec(shape, lambda b, i: (0,) * len(shape))
    return pl.pallas_call(
        _mlaprep_kernel,
        grid=(B, S // tm),
        in_specs=[pl.BlockSpec((1, H * QPAD, tm), lambda b, i: (b, 0, i)),
                  col(P_CKV // KV_RANK, KV_RANK), col(P_KR // LANES, LANES),
                  pl.BlockSpec((1, 1, tm), lambda b, i: (b, 0, i)),
                  const((half, LANES)), const((1, KV_RANK)), const((QPAD, LANES)),
                  const((1, NOPE)), const((1, LANES)),
                  const((KV_RANK, H * NOPE)), const((H * DV, KV_RANK))],
        out_specs=[pl.BlockSpec((1, H * QPAD, tm), lambda b, i: (b, 0, i)),
                   pl.BlockSpec((1, tm, H * QPAD), lambda b, i: (b, i, 0)),
                   pl.BlockSpec((1, H * DV, tm), lambda b, i: (b, 0, i))],
        out_shape=[jax.ShapeDtypeStruct((B, H * QPAD, S), bf16),
                   jax.ShapeDtypeStruct((B, S, H * QPAD), bf16),
                   jax.ShapeDtypeStruct((B, H * DV, S), bf16)],
        compiler_params=_cparams(2),
        name="mlaprep",
    )(q_t, proj, proj, positions.reshape(B, 1, S), inv_c,
      kv_norm_w.reshape(1, KV_RANK), qw, kwn, kwr, w_uk.astype(bf16), w_uv.T.astype(bf16))


def _attn_kernel(q_ref, k_ref, v_ref, o_ref, m_ref, acc_ref, s_ref):
    G = ATT_G
    i = pl.program_id(2)
    m_ref[...] = jnp.full_like(m_ref, -jnp.inf)
    acc_ref[...] = jnp.zeros_like(acc_ref)
    last = ((i + 1) * TQ - 1) // TK

    def qk(j, slot):
        start = pl.multiple_of(j * TK, TK)
        for g in range(G):
            s_ref[slot, g] = _dot(k_ref[0, pl.ds(start, TK), g * QPAD:(g + 1) * QPAD],
                                  q_ref[0, g * QPAD:(g + 1) * QPAD, :])

    def softmax_pv(j, slot, masked):
        start = pl.multiple_of(j * TK, TK)
        if masked:
            kc = (lax.broadcasted_iota(jnp.int32, (TK, TQ), 0) + j * TK) // CHUNK
            qc = (lax.broadcasted_iota(jnp.int32, (TK, TQ), 1) + i * TQ) // CHUNK
            visible = qc >= kc
        probs, alphas = [], []
        for g in range(G):
            s = s_ref[slot, g]
            if masked:
                s = jnp.where(visible, s, -jnp.inf)
            m_prev = m_ref[g]
            m_new = jnp.maximum(m_prev, jnp.max(s, axis=0, keepdims=True))
            alphas.append(jnp.exp2(m_prev - m_new))
            probs.append(jnp.exp2(s - m_new).astype(bf16))
            m_ref[g] = m_new
        ones = jnp.ones((DV, TK), bf16)
        for g in range(G):
            v1 = jnp.concatenate([v_ref[0, g * DV:(g + 1) * DV, pl.ds(start, TK)], ones], axis=0)
            acc_ref[g] = alphas[g] * acc_ref[g] + _dot(v1, probs[g])

    qk(0, 0)

    def body(p, carry):
        qk(2 * p + 1, 1)
        softmax_pv(2 * p, 0, False)
        qk(2 * p + 2, 0)
        softmax_pv(2 * p + 1, 1, False)
        return carry

    lax.fori_loop(0, last // 2, body, 0)

    @pl.when(last % 2 == 0)
    def _():
        softmax_pv(last, 0, True)

    @pl.when(last % 2 == 1)
    def _():
        qk(last, 1)
        softmax_pv(last - 1, 0, False)
        softmax_pv(last, 1, True)

    for g in range(G):
        acc = acc_ref[g]
        o_ref[0, :, g * DV:(g + 1) * DV] = (acc[:DV] / acc[DV:]).T.astype(o_ref.dtype)


def _attn(q_t, kp, v_t):
    B, S, _ = kp.shape
    G = ATT_G
    return pl.pallas_call(
        _attn_kernel,
        grid=(B, H // G, S // TQ),
        in_specs=[pl.BlockSpec((1, G * QPAD, TQ), lambda b, h, i: (b, h, i)),
                  pl.BlockSpec((1, S, G * QPAD), lambda b, h, i: (b, 0, h)),
                  pl.BlockSpec((1, G * DV, S), lambda b, h, i: (b, h, 0))],
        out_specs=pl.BlockSpec((1, TQ, G * DV), lambda b, h, i: (b, i, h)),
        out_shape=jax.ShapeDtypeStruct((B, S, H * DV), bf16),
        scratch_shapes=[pltpu.VMEM((G, 1, TQ), f32), pltpu.VMEM((G, 2 * DV, TQ), f32),
                        pltpu.VMEM((2, G, TK, TQ), f32)],
        compiler_params=_cparams(3),
        name="attn",
    )(q_t, kp, v_t)


def _merge_kernel(x_ref, oa_ref, ob_ref, gd_ref, gm_ref, mod_ref, wa_ref, wb_ref, wo_ref, o_ref):
    y_a = _dot(oa_ref[0], wa_ref[...])
    y_b = _dot(ob_ref[0], wb_ref[...])
    mix = _sigmoid(gd_ref[0].astype(f32)) * y_a + _sigmoid(gm_ref[0].astype(f32)) * y_b
    o_ref[0] = x_ref[0] + mod_ref[0, 2:3, :] * _dot(mix.astype(bf16), wo_ref[...])


def _merge(x, o_a, o_b, proj, mod3, w_a, w_b, w_o):
    B, S, _ = x.shape
    tm = TM_MERGE
    row = lambda idx: pl.BlockSpec((1, tm, D), lambda b, i: (b, i, idx))
    wspec = pl.BlockSpec((D, D), lambda b, i: (0, 0))
    return pl.pallas_call(
        _merge_kernel,
        grid=(B, S // tm),
        in_specs=[row(0), row(0), row(0), row(P_GDN // D), row(P_GMLA // D),
                  pl.BlockSpec((1, 6, D), lambda b, i: (b, 0, 0)), wspec, wspec, wspec],
        out_specs=row(0),
        out_shape=jax.ShapeDtypeStruct((B, S, D), f32),
        compiler_params=_cparams(2),
        name="merge",
    )(x, o_a, o_b, proj, proj, mod3, w_a.astype(bf16), w_b.astype(bf16), w_o.astype(bf16))


def _ffn_kernel(x_ref, halo_ref, mod_ref, nw_ref, wa_ref, wv_ref, cw_ref, cb_ref, wd_ref, o_ref,
                h_ref, a_ref, acc_ref):
    tm = x_ref.shape[1]
    i = pl.program_id(1)
    j = pl.program_id(2)

    @pl.when(j == 0)
    def _():
        shift, scale = mod_ref[0, 3:4, :], mod_ref[0, 4:5, :]
        hh = _modulated_norm(halo_ref[0], nw_ref[...], shift, scale)
        h_ref[0:SUBLANES, :] = jnp.where(i > 0, hh, 0.0).astype(bf16)
        h_ref[SUBLANES:, :] = _modulated_norm(x_ref[0], nw_ref[...], shift, scale).astype(bf16)
        acc_ref[...] = jnp.zeros_like(acc_ref)

    a_ref[...] = _dot(h_ref[...], wa_ref[...])
    v = _dot(h_ref[SUBLANES:, :], wv_ref[...])
    a = cb_ref[...] + cw_ref[2:3, :] * a_ref[8:8 + tm, :]
    for t in range(1, FFN_CONV):
        a = a + cw_ref[2 - t:3 - t, :] * a_ref[8 - t:8 - t + tm, :]
    gelu = 0.5 * a * (1.0 + lax.erf(a * (2.0 ** -0.5)))
    acc_ref[...] += _dot((gelu * v).astype(bf16), wd_ref[...])

    @pl.when(j == pl.num_programs(2) - 1)
    def _():
        o_ref[0] = x_ref[0] + mod_ref[0, 5:6, :] * acc_ref[...]


def _ffn(x, mod3, norm_w, w_up, conv_w, conv_b, w_down):
    B, S, _ = x.shape
    tm, tf = TM_FFN, TF_FFN
    nf = D_FF // tf
    hb = tm // SUBLANES
    return pl.pallas_call(
        _ffn_kernel,
        grid=(B, S // tm, nf),
        in_specs=[pl.BlockSpec((1, tm, D), lambda b, i, j: (b, i, 0)),
                  pl.BlockSpec((1, SUBLANES, D), lambda b, i, j: (b, jnp.maximum(i * hb - 1, 0), 0)),
                  pl.BlockSpec((1, 6, D), lambda b, i, j: (b, 0, 0)),
                  pl.BlockSpec((1, D), lambda b, i, j: (0, 0)),
                  pl.BlockSpec((D, tf), lambda b, i, j: (0, j)),
                  pl.BlockSpec((D, tf), lambda b, i, j: (0, nf + j)),
                  pl.BlockSpec((FFN_CONV, tf), lambda b, i, j: (0, j)),
                  pl.BlockSpec((1, tf), lambda b, i, j: (0, j)),
                  pl.BlockSpec((tf, D), lambda b, i, j: (j, 0))],
        out_specs=pl.BlockSpec((1, tm, D), lambda b, i, j: (b, i, 0)),
        out_shape=jax.ShapeDtypeStruct((B, S, D), f32),
        scratch_shapes=[pltpu.VMEM((tm + SUBLANES, D), bf16),
                        pltpu.VMEM((tm + SUBLANES, tf), f32),
                        pltpu.VMEM((tm, D), f32)],
        compiler_params=_cparams(3),
        name="ffn",
    )(x, x, mod3, norm_w, w_up, w_up, conv_w, conv_b.reshape(1, D_FF), w_down)


def kernel(x, c, positions, w_ada, b_ada, norm1_w, w_in, dn_conv_w, dn_a_log, dn_dt_bias, dn_norm_w,
           mla_kv_norm_w, mla_w_uk, mla_w_uv, mla_q_norm_w, mla_k_norm_w, w_out_dn, w_out_mla, w_o,
           norm2_w, ffn_w_up, ffn_conv_w, ffn_conv_b, ffn_w_down):
    B = x.shape[0]
    for l in range(w_ada.shape[0]):
        mod3 = _mod(c, w_ada[l], b_ada[l]).reshape(B, 6, D)
        w_p, w_qt = _pack_w_in(w_in[l])
        proj, q_t = _inproj(x, mod3, norm1_w[l].reshape(1, D), w_p, w_qt)
        o_a = _deltanet(proj, dn_conv_w[l], dn_a_log[l], dn_dt_bias[l], dn_norm_w[l])
        qp_t, kp, v_t = _mlaprep(proj, q_t, positions, mla_kv_norm_w[l], mla_w_uk[l], mla_w_uv[l],
                                 mla_q_norm_w[l], mla_k_norm_w[l])
        o_b = _attn(qp_t, kp, v_t)
        x = _merge(x, o_a, o_b, proj, mod3, w_out_dn[l], w_out_mla[l], w_o[l])
        x = _ffn(x, mod3, norm2_w[l].reshape(1, D), ffn_w_up[l].astype(bf16), ffn_conv_w[l],
                 ffn_conv_b[l], ffn_w_down[l].astype(bf16))
    return x
```

```python
import jax
import jax.numpy as jnp
import numpy as np
from jax import lax
from jax.experimental import pallas as pl
from jax.experimental.pallas import tpu as pltpu

f32 = jnp.float32
bf16 = jnp.bfloat16

D = 1024
EPS = 1e-6
CHUNK = 64
H = 8
DK = 128
NOPE, ROPE, DV = 128, 64, 128
QK_DIM = NOPE + ROPE
KV_RANK = 256
ROPE_THETA = 10000.0
D_FF = 2816
DN_CONV = 4
FFN_CONV = 3
LOG2E = 1.4426950408889634

LANES = 128
SUBLANES = 8
VMEM_LIMIT = 56 * 1024 * 1024

QPAD = 2 * LANES
P_DNQKV = 0
P_Z = 3 * D
P_GDN = 4 * D
P_GMLA = 5 * D
P_CKV = 6 * D
P_KR = P_CKV + KV_RANK
P_AB = P_KR + LANES
NP = P_AB + LANES

DN_C = 128
DN_NB = 2
TM_IN, TN_IN = 1024, 1664
TM_PREP = 512
TQ = 512
TK = 512
ATT_G = 4
TM_MERGE = 512
TM_FFN, TF_FFN = 1024, 1408


def _cparams(n_axes):
    return pltpu.CompilerParams(dimension_semantics=("arbitrary",) * n_axes,
                                vmem_limit_bytes=VMEM_LIMIT)


def _sigmoid(x):
    return 0.5 * (1.0 + jnp.tanh(0.5 * x))


def _silu(x):
    return x * _sigmoid(x)


def _dot(a, b):
    return jnp.dot(a, b, preferred_element_type=f32)


def _dot_nt(a, b):
    return lax.dot_general(a, b, (((1,), (1,)), ((), ())), preferred_element_type=f32)


def _dot_tn(a, b):
    return lax.dot_general(a, b, (((0,), (0,)), ((), ())), preferred_element_type=f32)


def _lane_tile(t, width):
    return jnp.concatenate([t] * (width // LANES), axis=1)


def _mod_kernel(c_ref, w_ref, b_ref, o_ref):
    a = _silu(c_ref[...]).astype(bf16)
    o_ref[...] = _dot(a, w_ref[...].astype(bf16)) + b_ref[...]


def _mod(c, w_ada, b_ada):
    B = c.shape[0]
    n = w_ada.shape[1]
    return pl.pallas_call(
        _mod_kernel,
        grid=(n // D,),
        in_specs=[pl.BlockSpec((B, D), lambda j: (0, 0)),
                  pl.BlockSpec((D, D), lambda j: (0, j)),
                  pl.BlockSpec((1, D), lambda j: (0, j))],
        out_specs=pl.BlockSpec((B, D), lambda j: (0, j)),
        out_shape=jax.ShapeDtypeStruct((B, n), f32),
        compiler_params=_cparams(1),
        name="mod",
    )(c, w_ada, b_ada.reshape(1, n))


def _pack_w_in(w_in):
    o = np.cumsum((0, D, D, D, D, H, H, H * QK_DIM, KV_RANK, ROPE, D, D))
    dn_qkv = w_in[:, o[0]:o[3]]
    dn_z = w_in[:, o[3]:o[4]]
    ab = w_in[:, o[4]:o[6]]
    mq = w_in[:, o[6]:o[7]]
    ckv = w_in[:, o[7]:o[8]]
    kr = w_in[:, o[8]:o[9]]
    g_dn = w_in[:, o[9]:o[10]]
    g_mla = w_in[:, o[10]:o[11]]
    kr = jnp.pad(kr, ((0, 0), (0, LANES - ROPE)))
    ab = jnp.pad(ab, ((0, 0), (0, LANES - 2 * H)))
    w_p = jnp.concatenate([dn_qkv, dn_z, g_dn, g_mla, ckv, kr, ab], axis=1).astype(bf16)
    return w_p, mq.T.astype(bf16)


def _modulated_norm(x, nw, shift, scale):
    r = lax.rsqrt(jnp.mean(x * x, axis=-1, keepdims=True) + EPS)
    return (x * r * nw) * (1.0 + scale) + shift


def _inproj_kernel(x_ref, mod_ref, nw_ref, w_ref, wq_ref, o_ref, qt_ref, h_ref):
    @pl.when(pl.program_id(2) == 0)
    def _():
        h = _modulated_norm(x_ref[0], nw_ref[...], mod_ref[0, 0:1, :], mod_ref[0, 1:2, :])
        h_ref[...] = h.astype(bf16)
        qt_ref[0] = _dot_nt(wq_ref[...], h_ref[...]).astype(qt_ref.dtype)

    o_ref[0] = _dot(h_ref[...], w_ref[...]).astype(o_ref.dtype)


def _inproj(x, mod3, norm_w, w_p, w_qt):
    B, S, _ = x.shape
    return pl.pallas_call(
        _inproj_kernel,
        grid=(B, S // TM_IN, NP // TN_IN),
        in_specs=[pl.BlockSpec((1, TM_IN, D), lambda b, i, j: (b, i, 0)),
                  pl.BlockSpec((1, 6, D), lambda b, i, j: (b, 0, 0)),
                  pl.BlockSpec((1, D), lambda b, i, j: (0, 0)),
                  pl.BlockSpec((D, TN_IN), lambda b, i, j: (0, j)),
                  pl.BlockSpec((H * QK_DIM, D), lambda b, i, j: (0, 0))],
        out_specs=[pl.BlockSpec((1, TM_IN, TN_IN), lambda b, i, j: (b, i, j)),
                   pl.BlockSpec((1, H * QK_DIM, TM_IN), lambda b, i, j: (b, 0, i))],
        out_shape=[jax.ShapeDtypeStruct((B, S, NP), bf16),
                   jax.ShapeDtypeStruct((B, H * QK_DIM, S), bf16)],
        scratch_shapes=[pltpu.VMEM((TM_IN, D), bf16)],
        compiler_params=_cparams(3),
        name="inproj",
    )(x, mod3, norm_w, w_p, w_qt)


def _cumsum_rows(x):
    n = x.shape[0]
    row = lax.broadcasted_iota(jnp.int32, x.shape, 0)
    s = 1
    while s < n:
        x = x + jnp.where(row >= s, pltpu.roll(x, s, 0), 0.0)
        s *= 2
    return x


def _pair_cols(t, ca, cb, rows):
    return jnp.concatenate([jnp.broadcast_to(t[:, ca:ca + 1], (rows, LANES)),
                            jnp.broadcast_to(t[:, cb:cb + 1], (rows, LANES))], axis=1)


def _block_diag_rows(x):
    z = jnp.zeros((x.shape[0], LANES), x.dtype)
    return jnp.concatenate([jnp.concatenate([x[:, :LANES], z], axis=1),
                            jnp.concatenate([z, x[:, LANES:]], axis=1)], axis=0)


def _doubling_masks(n, block):
    r = lax.broadcasted_iota(jnp.int32, (n, n), 0)
    c = lax.broadcasted_iota(jnp.int32, (n, n), 1)
    rx = r ^ c
    eye = rx == 0
    pair_mask = (rx == 1) & ((r & 1) == 1)
    levels = []
    s = 2
    while s < block:
        levels.append((rx >= s) & (rx < 2 * s) & ((r & s) != 0))
        s *= 2
    return eye, pair_mask, levels


def _deltanet_kernel(q_ref, k_ref, v_ref, z_ref, ab_ref, cw_ref, alog_ref, dtb_ref, nw_ref,
                     o_ref, xpad_ref, state_ref):
    C = DN_C
    P2 = 2 * LANES
    c = pl.program_id(1)

    @pl.when(c == 0)
    def _():
        xpad_ref[:, C:2 * C, :] = jnp.zeros((DN_NB, C, 3 * D), xpad_ref.dtype)
        state_ref[...] = jnp.zeros_like(state_ref)

    xpad_ref[:, 0:C, :] = xpad_ref[:, C:2 * C, :]
    xpad_ref[:, C:2 * C, 0:D] = q_ref[...]
    xpad_ref[:, C:2 * C, D:2 * D] = k_ref[...]
    xpad_ref[:, C:2 * C, 2 * D:3 * D] = v_ref[...]

    sr = lax.broadcasted_iota(jnp.int32, ((DN_CONV - 1) * C, 2 * C), 0)
    sc = lax.broadcasted_iota(jnp.int32, ((DN_CONV - 1) * C, 2 * C), 1)
    shift = jnp.where(sc == C + sr % C - (sr // C + 1), 1.0, 0.0).astype(bf16)

    row = lax.broadcasted_iota(jnp.int32, (C, P2), 0)
    col = lax.broadcasted_iota(jnp.int32, (C, P2), 1) & (LANES - 1)
    tri = row >= col
    strict = row > col
    masks = _doubling_masks(P2, C)
    r2 = lax.broadcasted_iota(jnp.int32, (P2, P2), 0)
    c2 = lax.broadcasted_iota(jnp.int32, (P2, P2), 1)
    same_head = (r2 < LANES) == (c2 < LANES)

    shifted_all = {(bi, col0): _dot(shift, xpad_ref[bi, :, col0:col0 + P2])
                   for bi in range(DN_NB) for col0 in range(0, 3 * D, P2)}

    def conv_silu(bi, col0):
        sl = slice(col0, col0 + P2)
        xa = xpad_ref[bi, :, sl]
        shifted = shifted_all[bi, col0]
        y = cw_ref[3:4, sl] * xa[C:].astype(f32)
        for j in range(1, DN_CONV):
            y = y + cw_ref[3 - j:4 - j, sl] * shifted[(j - 1) * C:j * C]
        return _silu(y)

    def l2n(t):
        t2 = t * t
        ra = lax.rsqrt(jnp.sum(t2[:, :LANES], axis=-1, keepdims=True) + EPS)
        rb = lax.rsqrt(jnp.sum(t2[:, LANES:], axis=-1, keepdims=True) + EPS)
        return jnp.concatenate([t[:, :LANES] * ra, t[:, LANES:] * rb], axis=1)

    n_pairs = H // 2
    chains = [(bi, p) for bi in range(DN_NB) for p in range(n_pairs)]
    decays = []
    for bi in range(DN_NB):
        ab = ab_ref[bi].astype(f32)
        x = ab + dtb_ref[...]
        softplus = jnp.maximum(x, 0.0) + jnp.log1p(jnp.exp(-jnp.abs(x)))
        gc = _cumsum_rows(-jnp.exp(alog_ref[...]) * softplus)
        decays.append((gc, gc.T, _sigmoid(ab)))
    pr = []
    for bi, p in chains:
        gc, gc_t, beta_t = decays[bi]
        ha, hb = 2 * p, 2 * p + 1
        q = l2n(conv_silu(bi, p * P2)) * (DK ** -0.5)
        k = l2n(conv_silu(bi, D + p * P2))
        v = conv_silu(bi, 2 * D + p * P2)
        gcb = _pair_cols(gc, ha, hb, C)
        betab = _pair_cols(beta_t, H + ha, H + hb, C)
        gcr = jnp.concatenate([jnp.broadcast_to(gc_t[ha:ha + 1, :], (C, LANES)),
                               jnp.broadcast_to(gc_t[hb:hb + 1, :], (C, LANES))], axis=1)
        decay = jnp.exp(jnp.where(tri, gcb - gcr, -1e30))
        eg = jnp.exp(gcb)
        gl = jnp.concatenate([jnp.broadcast_to(gc[C - 1:C, ha:ha + 1], (C, LANES)),
                              jnp.broadcast_to(gc[C - 1:C, hb:hb + 1], (C, LANES))], axis=1)
        kb = k * betab
        vb = v * betab
        gram = _dot_nt(jnp.concatenate([kb, q], axis=0).astype(bf16),
                       _block_diag_rows(k).astype(bf16))
        kbe = kb * eg
        rhs = jnp.concatenate([jnp.concatenate([vb[:, :LANES], kbe[:, :LANES]], axis=1),
                               jnp.concatenate([vb[:, LANES:], kbe[:, LANES:]], axis=1)], axis=0).astype(bf16)
        pr.append(dict(
            a_bd=_block_diag_rows(jnp.where(strict, gram[:C] * decay, 0.0)),
            intra=jnp.where(tri, gram[C:] * decay, 0.0).astype(bf16),
            rhs=rhs, qe=q * eg, k_dec=(k * jnp.exp(gl - gcb)).astype(bf16),
            dec_rows=jnp.concatenate(
                [jnp.broadcast_to(jnp.exp(gc[C - 1:C, ha:ha + 1]), (LANES, P2)),
                 jnp.broadcast_to(jnp.exp(gc[C - 1:C, hb:hb + 1]), (LANES, P2))], axis=0)))

    eye, pair_mask, level_masks = masks
    level_sel = [jnp.where(m, 1.0, 0.0).astype(bf16) for m in level_masks]
    a16s = [d["a_bd"].astype(bf16) for d in pr]
    ts = [(jnp.where(eye, 1.0, 0.0) - jnp.where(pair_mask, d["a_bd"], 0.0)).astype(bf16) for d in pr]
    for sel in level_sel:
        inner = [_dot(a16 * sel, t).astype(bf16) for a16, t in zip(a16s, ts)]
        ts = [t - _dot(t, x).astype(bf16) for t, x in zip(ts, inner)]

    sols = [_dot(t, d["rhs"]) for t, d in zip(ts, pr)]
    us = [jnp.concatenate([s[:C, :LANES], s[C:, :LANES]], axis=1) for s in sols]
    ws = [jnp.concatenate([s[:C, LANES:], s[C:, LANES:]], axis=1) for s in sols]
    s2s = [state_ref[n] for n in range(len(chains))]
    wqs = [_dot(jnp.concatenate([w, d["qe"]], axis=0).astype(bf16), s2.astype(bf16))
           for w, d, s2 in zip(ws, pr, s2s)]
    v_news = [u - wq[:C] for u, wq in zip(us, wqs)]
    outs = [wq[C:] + _dot(d["intra"], _block_diag_rows(vn).astype(bf16)) for wq, d, vn in zip(wqs, pr, v_news)]
    kvs = [_dot_tn(d["k_dec"], vn.astype(bf16)) for d, vn in zip(pr, v_news)]
    for n, (bi, p) in enumerate(chains):
        state_ref[n] = s2s[n] * pr[n]["dec_rows"] + jnp.where(same_head, kvs[n], 0.0)
        o = outs[n]
        o2 = o * o
        ra = lax.rsqrt(jnp.mean(o2[:, :LANES], axis=-1, keepdims=True) + EPS)
        rb = lax.rsqrt(jnp.mean(o2[:, LANES:], axis=-1, keepdims=True) + EPS)
        on = jnp.concatenate([o[:, :LANES] * ra * nw_ref[...], o[:, LANES:] * rb * nw_ref[...]], axis=1)
        zz = z_ref[bi, :, p * P2:(p + 1) * P2].astype(f32)
        o_ref[bi, :, p * P2:(p + 1) * P2] = (on * _silu(zz)).astype(o_ref.dtype)


def _deltanet(proj, conv_w, a_log, dt_bias, norm_w):
    B, S, _ = proj.shape
    C = DN_C
    pad = lambda t: jnp.pad(t.reshape(1, H), ((0, 0), (0, LANES - H)))
    col = lambda idx, width: pl.BlockSpec((DN_NB, C, width), lambda b, c: (b, c, idx))
    const = lambda shape: pl.BlockSpec(shape, lambda b, c: (0,) * len(shape))
    return pl.pallas_call(
        _deltanet_kernel,
        grid=(B // DN_NB, S // C),
        in_specs=[col(0, D), col(1, D), col(2, D), col(P_Z // D, D), col(P_AB // LANES, LANES),
                  const((DN_CONV, 3 * D)), const((1, LANES)), const((1, LANES)), const((1, DK))],
        out_specs=pl.BlockSpec((DN_NB, C, D), lambda b, c: (b, c, 0)),
        out_shape=jax.ShapeDtypeStruct((B, S, D), bf16),
        scratch_shapes=[pltpu.VMEM((DN_NB, 2 * C, 3 * D), proj.dtype),
                        pltpu.VMEM((DN_NB * (H // 2), 2 * LANES, 2 * LANES), f32)],
        compiler_params=_cparams(2),
        name="deltanet",
    )(proj, proj, proj, proj, proj, conv_w, pad(a_log), pad(dt_bias), norm_w.reshape(1, DK))


def _rope(r, cos, sin_lo, sin_hi):
    return r * cos + pltpu.roll(r, LANES - ROPE // 2, 1) * sin_lo + pltpu.roll(r, ROPE // 2, 1) * sin_hi


def _mlaprep_kernel(qt_ref, ckv_ref, kr_ref, pos_ref, inv_ref, kvw_ref, qw_ref,
                    kwn_ref, kwr_ref, wuk_ref, wuvt_ref, qo_ref, ko_ref, vo_ref):
    tm = ckv_ref.shape[1]
    half = ROPE // 2

    ang_t = _lane_tile(inv_ref[...], tm) * pos_ref[0].astype(f32)
    cos_t = jnp.cos(ang_t)
    sin_t = jnp.sin(ang_t)
    z_half = jnp.zeros((half, tm), f32)
    z_pad = jnp.zeros((LANES - ROPE, tm), f32)
    cos = jnp.concatenate([cos_t, cos_t, z_pad], axis=0).T
    sin_lo = jnp.concatenate([-sin_t, z_half, z_pad], axis=0).T
    sin_hi = jnp.concatenate([z_half, sin_t, z_pad], axis=0).T

    ckv = ckv_ref[0].astype(f32)
    ckv_n = (ckv * lax.rsqrt(jnp.mean(ckv * ckv, axis=-1, keepdims=True) + EPS) * kvw_ref[...]).astype(bf16)
    k_nope = _dot(ckv_n, wuk_ref[...])
    vo_ref[0] = _dot_nt(wuvt_ref[...], ckv_n).astype(vo_ref.dtype)

    kr = kr_ref[0].astype(f32)
    kr_ss = jnp.sum(kr * kr, axis=-1, keepdims=True)
    kr_rot = _rope(kr * kwr_ref[...], cos, sin_lo, sin_hi)
    for h in range(H):
        kn = k_nope[:, h * NOPE:(h + 1) * NOPE]
        rk = lax.rsqrt((jnp.sum(kn * kn, axis=-1, keepdims=True) + kr_ss) * (1.0 / QK_DIM) + EPS)
        ko_ref[0, :, h * QPAD:h * QPAD + NOPE] = (kn * rk * kwn_ref[...]).astype(ko_ref.dtype)
        ko_ref[0, :, h * QPAD + NOPE:(h + 1) * QPAD] = (kr_rot * rk).astype(ko_ref.dtype)

    qw_t = _lane_tile(qw_ref[...], tm)
    for h in range(H):
        qh = qt_ref[0, h * QK_DIM:(h + 1) * QK_DIM, :].astype(f32)
        rq = lax.rsqrt(jnp.sum(qh * qh, axis=0, keepdims=True) * (1.0 / QK_DIM) + EPS) * (QK_DIM ** -0.5 * LOG2E)
        qn = qh * rq * qw_t
        x1 = qn[NOPE:NOPE + half]
        x2 = qn[NOPE + half:QK_DIM]
        r0 = h * QPAD
        qo_ref[0, r0:r0 + NOPE, :] = qn[:NOPE].astype(qo_ref.dtype)
        qo_ref[0, r0 + NOPE:r0 + NOPE + half, :] = (x1 * cos_t - x2 * sin_t).astype(qo_ref.dtype)
        qo_ref[0, r0 + NOPE + half:r0 + QK_DIM, :] = (x2 * cos_t + x1 * sin_t).astype(qo_ref.dtype)
        qo_ref[0, r0 + QK_DIM:r0 + QPAD, :] = jnp.zeros((QPAD - QK_DIM, tm), qo_ref.dtype)


def _mlaprep(proj, q_t, positions, kv_norm_w, w_uk, w_uv, q_norm_w, k_norm_w):
    B, S, _ = proj.shape
    tm = TM_PREP
    half = ROPE // 2
    inv = ROPE_THETA ** (-jnp.arange(half, dtype=f32) / half)
    inv_c = jnp.broadcast_to(inv.reshape(half, 1), (half, LANES))
    qw = jnp.broadcast_to(q_norm_w.reshape(QK_DIM, 1), (QK_DIM, LANES))
    kwn = k_norm_w[:NOPE].reshape(1, NOPE)
    kwr = jnp.pad(k_norm_w[NOPE:], (0, LANES - ROPE)).reshape(1, LANES)
    col = lambda idx, width: pl.BlockSpec((1, tm, width), lambda b, i: (b, i, idx))
    const = lambda shape: pl.BlockSpec(shape, lambda b, i: (0,) * len(shape))
    return pl.pallas_call(
        _mlaprep_kernel,
        grid=(B, S // tm),
        in_specs=[pl.BlockSpec((1, H * QK_DIM, tm), lambda b, i: (b, 0, i)),
                  col(P_CKV // KV_RANK, KV_RANK), col(P_KR // LANES, LANES),
                  pl.BlockSpec((1, 1, tm), lambda b, i: (b, 0, i)),
                  const((half, LANES)), const((1, KV_RANK)), const((QK_DIM, LANES)),
                  const((1, NOPE)), const((1, LANES)),
                  const((KV_RANK, H * NOPE)), const((H * DV, KV_RANK))],
        out_specs=[pl.BlockSpec((1, H * QPAD, tm), lambda b, i: (b, 0, i)),
                   pl.BlockSpec((1, tm, H * QPAD), lambda b, i: (b, i, 0)),
                   pl.BlockSpec((1, H * DV, tm), lambda b, i: (b, 0, i))],
        out_shape=[jax.ShapeDtypeStruct((B, H * QPAD, S), bf16),
                   jax.ShapeDtypeStruct((B, S, H * QPAD), bf16),
                   jax.ShapeDtypeStruct((B, H * DV, S), bf16)],
        compiler_params=_cparams(2),
        name="mlaprep",
    )(q_t, proj, proj, positions.reshape(B, 1, S), inv_c,
      kv_norm_w.reshape(1, KV_RANK), qw, kwn, kwr, w_uk.astype(bf16), w_uv.T.astype(bf16))


def _attn_kernel(q_ref, k_ref, v_ref, o_ref, m_ref, acc_ref, s_ref):
    G = ATT_G
    i = pl.program_id(2)
    m_ref[...] = jnp.full_like(m_ref, -jnp.inf)
    acc_ref[...] = jnp.zeros_like(acc_ref)
    last = ((i + 1) * TQ - 1) // TK

    def qk(j, slot):
        start = pl.multiple_of(j * TK, TK)
        for g in range(G):
            s_ref[slot, g] = _dot(k_ref[0, pl.ds(start, TK), g * QPAD:(g + 1) * QPAD],
                                  q_ref[0, g * QPAD:(g + 1) * QPAD, :])

    def softmax_pv(j, slot, masked):
        start = pl.multiple_of(j * TK, TK)
        if masked:
            kc = (lax.broadcasted_iota(jnp.int32, (TK, TQ), 0) + j * TK) // CHUNK
            qc = (lax.broadcasted_iota(jnp.int32, (TK, TQ), 1) + i * TQ) // CHUNK
            visible = qc >= kc
        probs, alphas = [], []
        for g in range(G):
            s = s_ref[slot, g]
            if masked:
                s = jnp.where(visible, s, -jnp.inf)
            m_prev = m_ref[g]
            m_new = jnp.maximum(m_prev, jnp.max(s, axis=0, keepdims=True))
            alphas.append(jnp.exp2(m_prev - m_new))
            probs.append(jnp.exp2(s - m_new).astype(bf16))
            m_ref[g] = m_new
        ones = jnp.ones((DV, TK), bf16)
        for g in range(G):
            v1 = jnp.concatenate([v_ref[0, g * DV:(g + 1) * DV, pl.ds(start, TK)], ones], axis=0)
            acc_ref[g] = alphas[g] * acc_ref[g] + _dot(v1, probs[g])

    qk(0, 0)

    def body(p, carry):
        qk(2 * p + 1, 1)
        softmax_pv(2 * p, 0, False)
        qk(2 * p + 2, 0)
        softmax_pv(2 * p + 1, 1, False)
        return carry

    lax.fori_loop(0, last // 2, body, 0)

    @pl.when(last % 2 == 0)
    def _():
        softmax_pv(last, 0, True)

    @pl.when(last % 2 == 1)
    def _():
        qk(last, 1)
        softmax_pv(last - 1, 0, False)
        softmax_pv(last, 1, True)

    for g in range(G):
        acc = acc_ref[g]
        o_ref[0, :, g * DV:(g + 1) * DV] = (acc[:DV] / acc[DV:]).T.astype(o_ref.dtype)


def _attn(q_t, kp, v_t):
    B, S, _ = kp.shape
    G = ATT_G
    return pl.pallas_call(
        _attn_kernel,
        grid=(B, H // G, S // TQ),
        in_specs=[pl.BlockSpec((1, G * QPAD, TQ), lambda b, h, i: (b, h, i)),
                  pl.BlockSpec((1, S, G * QPAD), lambda b, h, i: (b, 0, h)),
                  pl.BlockSpec((1, G * DV, S), lambda b, h, i: (b, h, 0))],
        out_specs=pl.BlockSpec((1, TQ, G * DV), lambda b, h, i: (b, i, h)),
        out_shape=jax.ShapeDtypeStruct((B, S, H * DV), bf16),
        scratch_shapes=[pltpu.VMEM((G, 1, TQ), f32), pltpu.VMEM((G, 2 * DV, TQ), f32),
                        pltpu.VMEM((2, G, TK, TQ), f32)],
        compiler_params=_cparams(3),
        name="attn",
    )(q_t, kp, v_t)


def _merge_kernel(x_ref, oa_ref, ob_ref, gd_ref, gm_ref, mod_ref, wa_ref, wb_ref, wo_ref, o_ref):
    y_a = _dot(oa_ref[0], wa_ref[...])
    y_b = _dot(ob_ref[0], wb_ref[...])
    mix = _sigmoid(gd_ref[0].astype(f32)) * y_a + _sigmoid(gm_ref[0].astype(f32)) * y_b
    o_ref[0] = x_ref[0] + mod_ref[0, 2:3, :] * _dot(mix.astype(bf16), wo_ref[...])


def _merge(x, o_a, o_b, proj, mod3, w_a, w_b, w_o):
    B, S, _ = x.shape
    tm = TM_MERGE
    row = lambda idx: pl.BlockSpec((1, tm, D), lambda b, i: (b, i, idx))
    wspec = pl.BlockSpec((D, D), lambda b, i: (0, 0))
    return pl.pallas_call(
        _merge_kernel,
        grid=(B, S // tm),
        in_specs=[row(0), row(0), row(0), row(P_GDN // D), row(P_GMLA // D),
                  pl.BlockSpec((1, 6, D), lambda b, i: (b, 0, 0)), wspec, wspec, wspec],
        out_specs=row(0),
        out_shape=jax.ShapeDtypeStruct((B, S, D), f32),
        compiler_params=_cparams(2),
        name="merge",
    )(x, o_a, o_b, proj, proj, mod3, w_a.astype(bf16), w_b.astype(bf16), w_o.astype(bf16))


def _ffn_kernel(x_ref, halo_ref, mod_ref, nw_ref, wa_ref, wv_ref, cw_ref, cb_ref, wd_ref, o_ref,
                h_ref, a_ref, acc_ref):
    tm = x_ref.shape[1]
    i = pl.program_id(1)
    j = pl.program_id(2)

    @pl.when(j == 0)
    def _():
        shift, scale = mod_ref[0, 3:4, :], mod_ref[0, 4:5, :]
        hh = _modulated_norm(halo_ref[0], nw_ref[...], shift, scale)
        h_ref[0:SUBLANES, :] = jnp.where(i > 0, hh, 0.0).astype(bf16)
        h_ref[SUBLANES:, :] = _modulated_norm(x_ref[0], nw_ref[...], shift, scale).astype(bf16)
        acc_ref[...] = jnp.zeros_like(acc_ref)

    a_ref[...] = _dot(h_ref[...], wa_ref[...])
    v = _dot(h_ref[SUBLANES:, :], wv_ref[...])
    a = cb_ref[...] + cw_ref[2:3, :] * a_ref[8:8 + tm, :]
    for t in range(1, FFN_CONV):
        a = a + cw_ref[2 - t:3 - t, :] * a_ref[8 - t:8 - t + tm, :]
    gelu = 0.5 * a * (1.0 + lax.erf(a * (2.0 ** -0.5)))
    acc_ref[...] += _dot((gelu * v).astype(bf16), wd_ref[...])

    @pl.when(j == pl.num_programs(2) - 1)
    def _():
        o_ref[0] = x_ref[0] + mod_ref[0, 5:6, :] * acc_ref[...]


def _ffn(x, mod3, norm_w, w_up, conv_w, conv_b, w_down):
    B, S, _ = x.shape
    tm, tf = TM_FFN, TF_FFN
    nf = D_FF // tf
    hb = tm // SUBLANES
    return pl.pallas_call(
        _ffn_kernel,
        grid=(B, S // tm, nf),
        in_specs=[pl.BlockSpec((1, tm, D), lambda b, i, j: (b, i, 0)),
                  pl.BlockSpec((1, SUBLANES, D), lambda b, i, j: (b, jnp.maximum(i * hb - 1, 0), 0)),
                  pl.BlockSpec((1, 6, D), lambda b, i, j: (b, 0, 0)),
                  pl.BlockSpec((1, D), lambda b, i, j: (0, 0)),
                  pl.BlockSpec((D, tf), lambda b, i, j: (0, j)),
                  pl.BlockSpec((D, tf), lambda b, i, j: (0, nf + j)),
                  pl.BlockSpec((FFN_CONV, tf), lambda b, i, j: (0, j)),
                  pl.BlockSpec((1, tf), lambda b, i, j: (0, j)),
                  pl.BlockSpec((tf, D), lambda b, i, j: (j, 0))],
        out_specs=pl.BlockSpec((1, tm, D), lambda b, i, j: (b, i, 0)),
        out_shape=jax.ShapeDtypeStruct((B, S, D), f32),
        scratch_shapes=[pltpu.VMEM((tm + SUBLANES, D), bf16),
                        pltpu.VMEM((tm + SUBLANES, tf), f32),
                        pltpu.VMEM((tm, D), f32)],
        compiler_params=_cparams(3),
        name="ffn",
    )(x, x, mod3, norm_w, w_up, w_up, conv_w, conv_b.reshape(1, D_FF), w_down)


def kernel(x, c, positions, w_ada, b_ada, norm1_w, w_in, dn_conv_w, dn_a_log, dn_dt_bias, dn_norm_w,
           mla_kv_norm_w, mla_w_uk, mla_w_uv, mla_q_norm_w, mla_k_norm_w, w_out_dn, w_out_mla, w_o,
           norm2_w, ffn_w_up, ffn_conv_w, ffn_conv_b, ffn_w_down):
    B = x.shape[0]
    for l in range(w_ada.shape[0]):
        mod3 = _mod(c, w_ada[l], b_ada[l]).reshape(B, 6, D)
        w_p, w_qt = _pack_w_in(w_in[l])
        proj, q_t = _inproj(x, mod3, norm1_w[l].reshape(1, D), w_p, w_qt)
        o_a = _deltanet(proj, dn_conv_w[l], dn_a_log[l], dn_dt_bias[l], dn_norm_w[l])
        qp_t, kp, v_t = _mlaprep(proj, q_t, positions, mla_kv_norm_w[l], mla_w_uk[l], mla_w_uv[l],
                                 mla_q_norm_w[l], mla_k_norm_w[l])
        o_b = _attn(qp_t, kp, v_t)
        x = _merge(x, o_a, o_b, proj, mod3, w_out_dn[l], w_out_mla[l], w_o[l])
        x = _ffn(x, mod3, norm2_w[l].reshape(1, D), ffn_w_up[l].astype(bf16), ffn_conv_w[l],
                 ffn_conv_b[l], ffn_w_down[l].astype(bf16))
    return x
```

```python
import jax
import jax.numpy as jnp
import numpy as np
from jax import lax
from jax.experimental import pallas as pl
from jax.experimental.pallas import tpu as pltpu

f32 = jnp.float32
bf16 = jnp.bfloat16

D = 1024
EPS = 1e-6
CHUNK = 64
H = 8
DK = 128
NOPE, ROPE, DV = 128, 64, 128
QK_DIM = NOPE + ROPE
KV_RANK = 256
ROPE_THETA = 10000.0
D_FF = 2816
DN_CONV = 4
FFN_CONV = 3
LOG2E = 1.4426950408889634

LANES = 128
SUBLANES = 8
VMEM_LIMIT = 56 * 1024 * 1024

QPAD = 2 * LANES
P_DNQKV = 0
P_Z = 3 * D
P_GDN = 4 * D
P_GMLA = 5 * D
P_CKV = 6 * D
P_KR = P_CKV + KV_RANK
P_AB = P_KR + LANES
NP = P_AB + LANES

DN_C = 128
DN_NB = 2
DN_LEVELS = DN_C.bit_length() - 2
TM_IN, TN_IN = 1024, 1664
TM_PREP = 512
TQ = 512
TK = 512
ATT_G = 4
TM_MERGE = 1024
TM_FFN, TF_FFN = 1024, 1408


def _cparams(n_axes):
    return pltpu.CompilerParams(dimension_semantics=("arbitrary",) * n_axes,
                                vmem_limit_bytes=VMEM_LIMIT)


def _sigmoid(x):
    return 0.5 * (1.0 + jnp.tanh(0.5 * x))


def _silu(x):
    return x * _sigmoid(x)


def _dot(a, b):
    return jnp.dot(a, b, preferred_element_type=f32)


def _dot_nt(a, b):
    return lax.dot_general(a, b, (((1,), (1,)), ((), ())), preferred_element_type=f32)


def _dot_tn(a, b):
    return lax.dot_general(a, b, (((0,), (0,)), ((), ())), preferred_element_type=f32)


def _lane_tile(t, width):
    return jnp.concatenate([t] * (width // LANES), axis=1)


def _mod_kernel(c_ref, w_ref, b_ref, o_ref):
    a = _silu(c_ref[...]).astype(bf16)
    o_ref[...] = _dot(a, w_ref[...].astype(bf16)) + b_ref[...]


def _mod(c, w_ada, b_ada):
    B = c.shape[0]
    n = w_ada.shape[1]
    return pl.pallas_call(
        _mod_kernel,
        grid=(n // D,),
        in_specs=[pl.BlockSpec((B, D), lambda j: (0, 0)),
                  pl.BlockSpec((D, D), lambda j: (0, j)),
                  pl.BlockSpec((1, D), lambda j: (0, j))],
        out_specs=pl.BlockSpec((B, D), lambda j: (0, j)),
        out_shape=jax.ShapeDtypeStruct((B, n), f32),
        compiler_params=_cparams(1),
        name="mod",
    )(c, w_ada, b_ada.reshape(1, n))


def _pack_w_in(w_in):
    o = np.cumsum((0, D, D, D, D, H, H, H * QK_DIM, KV_RANK, ROPE, D, D))
    dn_qkv = w_in[:, o[0]:o[3]]
    dn_z = w_in[:, o[3]:o[4]]
    ab = w_in[:, o[4]:o[6]]
    mq = w_in[:, o[6]:o[7]]
    ckv = w_in[:, o[7]:o[8]]
    kr = w_in[:, o[8]:o[9]]
    g_dn = w_in[:, o[9]:o[10]]
    g_mla = w_in[:, o[10]:o[11]]
    kr = jnp.pad(kr, ((0, 0), (0, LANES - ROPE)))
    ab = jnp.pad(ab, ((0, 0), (0, LANES - 2 * H)))
    w_p = jnp.concatenate([dn_qkv, dn_z, g_dn, g_mla, ckv, kr, ab], axis=1).astype(bf16)
    return w_p, mq.T.astype(bf16)


def _modulated_norm(x, nw, shift, scale):
    r = lax.rsqrt(jnp.mean(x * x, axis=-1, keepdims=True) + EPS)
    return (x * r * nw) * (1.0 + scale) + shift


def _inproj_kernel(x_ref, mod_ref, nw_ref, w_ref, wq_ref, o_ref, qt_ref, h_ref):
    @pl.when(pl.program_id(2) == 0)
    def _():
        h = _modulated_norm(x_ref[0], nw_ref[...], mod_ref[0, 0:1, :], mod_ref[0, 1:2, :])
        h_ref[...] = h.astype(bf16)
        qt_ref[0] = _dot_nt(wq_ref[...], h_ref[...]).astype(qt_ref.dtype)

    o_ref[0] = _dot(h_ref[...], w_ref[...]).astype(o_ref.dtype)


def _inproj(x, mod3, norm_w, w_p, w_qt):
    B, S, _ = x.shape
    return pl.pallas_call(
        _inproj_kernel,
        grid=(B, S // TM_IN, NP // TN_IN),
        in_specs=[pl.BlockSpec((1, TM_IN, D), lambda b, i, j: (b, i, 0)),
                  pl.BlockSpec((1, 6, D), lambda b, i, j: (b, 0, 0)),
                  pl.BlockSpec((1, D), lambda b, i, j: (0, 0)),
                  pl.BlockSpec((D, TN_IN), lambda b, i, j: (0, j)),
                  pl.BlockSpec((H * QK_DIM, D), lambda b, i, j: (0, 0))],
        out_specs=[pl.BlockSpec((1, TM_IN, TN_IN), lambda b, i, j: (b, i, j)),
                   pl.BlockSpec((1, H * QK_DIM, TM_IN), lambda b, i, j: (b, 0, i))],
        out_shape=[jax.ShapeDtypeStruct((B, S, NP), bf16),
                   jax.ShapeDtypeStruct((B, H * QK_DIM, S), bf16)],
        scratch_shapes=[pltpu.VMEM((TM_IN, D), bf16)],
        compiler_params=_cparams(3),
        name="inproj",
    )(x, mod3, norm_w, w_p, w_qt)


def _cumsum_rows(x):
    n = x.shape[0]
    row = lax.broadcasted_iota(jnp.int32, x.shape, 0)
    s = 1
    while s < n:
        x = x + jnp.where(row >= s, pltpu.roll(x, s, 0), 0.0)
        s *= 2
    return x


def _pair_cols(t, ca, cb, rows):
    return jnp.concatenate([jnp.broadcast_to(t[:, ca:ca + 1], (rows, LANES)),
                            jnp.broadcast_to(t[:, cb:cb + 1], (rows, LANES))], axis=1)


def _block_diag_rows(x):
    z = jnp.zeros((x.shape[0], LANES), x.dtype)
    return jnp.concatenate([jnp.concatenate([x[:, :LANES], z], axis=1),
                            jnp.concatenate([z, x[:, LANES:]], axis=1)], axis=0)


def _doubling_masks(n, block):
    r = lax.broadcasted_iota(jnp.int32, (n, n), 0)
    c = lax.broadcasted_iota(jnp.int32, (n, n), 1)
    rx = r ^ c
    eye = rx == 0
    pair_mask = (rx == 1) & ((r & 1) == 1)
    levels = []
    s = 2
    while s < block:
        levels.append((rx >= s) & (rx < 2 * s) & ((r & s) != 0))
        s *= 2
    return eye, pair_mask, levels


def _deltanet_kernel(q_ref, k_ref, v_ref, z_ref, ab_ref, cw_ref, alog_ref, dtb_ref, nw_ref,
                     o_ref, xpad_ref, state_ref, sel_ref, shift_ref):
    C = DN_C
    P2 = 2 * LANES
    c = pl.program_id(1)

    @pl.when(c == 0)
    def _():
        xpad_ref[:, C:2 * C, :] = jnp.zeros((DN_NB, C, 3 * D), xpad_ref.dtype)
        state_ref[...] = jnp.zeros_like(state_ref)
        eye, pair_mask, level_masks = _doubling_masks(P2, C)
        for n, m in enumerate(level_masks + [eye, pair_mask]):
            sel_ref[n] = jnp.where(m, 1.0, 0.0).astype(bf16)
        sr = lax.broadcasted_iota(jnp.int32, ((DN_CONV - 1) * C, 2 * C), 0)
        sc = lax.broadcasted_iota(jnp.int32, ((DN_CONV - 1) * C, 2 * C), 1)
        shift_ref[...] = jnp.where(sc == C + sr % C - (sr // C + 1), 1.0, 0.0).astype(bf16)

    xpad_ref[:, 0:C, :] = xpad_ref[:, C:2 * C, :]
    xpad_ref[:, C:2 * C, 0:D] = q_ref[...]
    xpad_ref[:, C:2 * C, D:2 * D] = k_ref[...]
    xpad_ref[:, C:2 * C, 2 * D:3 * D] = v_ref[...]

    shift = shift_ref[...]

    row = lax.broadcasted_iota(jnp.int32, (C, P2), 0)
    col = lax.broadcasted_iota(jnp.int32, (C, P2), 1) & (LANES - 1)
    tri = row >= col
    strict = row > col
    r2 = lax.broadcasted_iota(jnp.int32, (P2, P2), 0)
    c2 = lax.broadcasted_iota(jnp.int32, (P2, P2), 1)
    same_head = (r2 < LANES) == (c2 < LANES)

    shifted_all = {(bi, col0): _dot(shift, xpad_ref[bi, :, col0:col0 + P2])
                   for bi in range(DN_NB) for col0 in range(0, 3 * D, P2)}

    def conv_silu(bi, col0):
        sl = slice(col0, col0 + P2)
        xa = xpad_ref[bi, :, sl]
        shifted = shifted_all[bi, col0]
        y = cw_ref[3:4, sl] * xa[C:].astype(f32)
        for j in range(1, DN_CONV):
            y = y + cw_ref[3 - j:4 - j, sl] * shifted[(j - 1) * C:j * C]
        return _silu(y)

    def l2n(t):
        t2 = t * t
        ra = lax.rsqrt(jnp.sum(t2[:, :LANES], axis=-1, keepdims=True) + EPS)
        rb = lax.rsqrt(jnp.sum(t2[:, LANES:], axis=-1, keepdims=True) + EPS)
        return jnp.concatenate([t[:, :LANES] * ra, t[:, LANES:] * rb], axis=1)

    n_pairs = H // 2
    chains = [(bi, p) for bi in range(DN_NB) for p in range(n_pairs)]
    decays = []
    for bi in range(DN_NB):
        ab = ab_ref[bi].astype(f32)
        x = ab + dtb_ref[...]
        softplus = jnp.maximum(x, 0.0) + jnp.log1p(jnp.exp(-jnp.abs(x)))
        gc = _cumsum_rows(-jnp.exp(alog_ref[...]) * softplus)
        decays.append((gc, gc.T, _sigmoid(ab)))
    pr = []
    for bi, p in chains:
        gc, gc_t, beta_t = decays[bi]
        ha, hb = 2 * p, 2 * p + 1
        q = l2n(conv_silu(bi, p * P2)) * (DK ** -0.5)
        k = l2n(conv_silu(bi, D + p * P2))
        v = conv_silu(bi, 2 * D + p * P2)
        gcb = _pair_cols(gc, ha, hb, C)
        betab = _pair_cols(beta_t, H + ha, H + hb, C)
        gcr = jnp.concatenate([jnp.broadcast_to(gc_t[ha:ha + 1, :], (C, LANES)),
                               jnp.broadcast_to(gc_t[hb:hb + 1, :], (C, LANES))], axis=1)
        decay = jnp.exp(jnp.where(tri, gcb - gcr, -1e30))
        eg = jnp.exp(gcb)
        gl = jnp.concatenate([jnp.broadcast_to(gc[C - 1:C, ha:ha + 1], (C, LANES)),
                              jnp.broadcast_to(gc[C - 1:C, hb:hb + 1], (C, LANES))], axis=1)
        kb = k * betab
        vb = v * betab
        gram = _dot_nt(jnp.concatenate([kb, q], axis=0).astype(bf16),
                       _block_diag_rows(k.astype(bf16)))
        kbe = kb * eg
        rhs = jnp.concatenate([jnp.concatenate([vb[:, :LANES], kbe[:, :LANES]], axis=1),
                               jnp.concatenate([vb[:, LANES:], kbe[:, LANES:]], axis=1)], axis=0).astype(bf16)
        pr.append(dict(
            a16=_block_diag_rows(jnp.where(strict, gram[:C] * decay, 0.0).astype(bf16)),
            intra=jnp.where(tri, gram[C:] * decay, 0.0).astype(bf16),
            rhs=rhs, qe=q * eg, k_dec=(k * jnp.exp(gl - gcb)).astype(bf16),
            dec_rows=jnp.concatenate(
                [jnp.broadcast_to(jnp.exp(gc[C - 1:C, ha:ha + 1]), (LANES, P2)),
                 jnp.broadcast_to(jnp.exp(gc[C - 1:C, hb:hb + 1]), (LANES, P2))], axis=0)))

    level_sel = [sel_ref[n] for n in range(DN_LEVELS)]
    eye16 = sel_ref[DN_LEVELS]
    pair_sel = sel_ref[DN_LEVELS + 1]
    a16s = [d["a16"] for d in pr]
    ts = [eye16 - a16 * pair_sel for a16 in a16s]
    for sel in level_sel:
        inner = [_dot(a16 * sel, t).astype(bf16) for a16, t in zip(a16s, ts)]
        ts = [t - _dot(t, x).astype(bf16) for t, x in zip(ts, inner)]

    sols = [_dot(t, d["rhs"]) for t, d in zip(ts, pr)]
    us = [jnp.concatenate([s[:C, :LANES], s[C:, :LANES]], axis=1) for s in sols]
    ws = [jnp.concatenate([s[:C, LANES:], s[C:, LANES:]], axis=1) for s in sols]
    s2s = [state_ref[n] for n in range(len(chains))]
    wqs = [_dot(jnp.concatenate([w, d["qe"]], axis=0).astype(bf16), s2.astype(bf16))
           for w, d, s2 in zip(ws, pr, s2s)]
    v_news = [u - wq[:C] for u, wq in zip(us, wqs)]
    vn16s = [vn.astype(bf16) for vn in v_news]
    outs = [wq[C:] + _dot(d["intra"], _block_diag_rows(vn16)) for wq, d, vn16 in zip(wqs, pr, vn16s)]
    kvs = [_dot_tn(d["k_dec"], vn16) for d, vn16 in zip(pr, vn16s)]
    for n, (bi, p) in enumerate(chains):
        state_ref[n] = s2s[n] * pr[n]["dec_rows"] + jnp.where(same_head, kvs[n], 0.0)
        o = outs[n]
        o2 = o * o
        ra = lax.rsqrt(jnp.mean(o2[:, :LANES], axis=-1, keepdims=True) + EPS)
        rb = lax.rsqrt(jnp.mean(o2[:, LANES:], axis=-1, keepdims=True) + EPS)
        on = jnp.concatenate([o[:, :LANES] * ra * nw_ref[...], o[:, LANES:] * rb * nw_ref[...]], axis=1)
        zz = z_ref[bi, :, p * P2:(p + 1) * P2].astype(f32)
        o_ref[bi, :, p * P2:(p + 1) * P2] = (on * _silu(zz)).astype(o_ref.dtype)


def _deltanet(proj, conv_w, a_log, dt_bias, norm_w):
    B, S, _ = proj.shape
    C = DN_C
    pad = lambda t: jnp.pad(t.reshape(1, H), ((0, 0), (0, LANES - H)))
    col = lambda idx, width: pl.BlockSpec((DN_NB, C, width), lambda b, c: (b, c, idx))
    const = lambda shape: pl.BlockSpec(shape, lambda b, c: (0,) * len(shape))
    return pl.pallas_call(
        _deltanet_kernel,
        grid=(B // DN_NB, S // C),
        in_specs=[col(0, D), col(1, D), col(2, D), col(P_Z // D, D), col(P_AB // LANES, LANES),
                  const((DN_CONV, 3 * D)), const((1, LANES)), const((1, LANES)), const((1, DK))],
        out_specs=pl.BlockSpec((DN_NB, C, D), lambda b, c: (b, c, 0)),
        out_shape=jax.ShapeDtypeStruct((B, S, D), bf16),
        scratch_shapes=[pltpu.VMEM((DN_NB, 2 * C, 3 * D), proj.dtype),
                        pltpu.VMEM((DN_NB * (H // 2), 2 * LANES, 2 * LANES), f32),
                        pltpu.VMEM((DN_LEVELS + 2, 2 * LANES, 2 * LANES), bf16),
                        pltpu.VMEM(((DN_CONV - 1) * C, 2 * C), bf16)],
        compiler_params=_cparams(2),
        name="deltanet",
    )(proj, proj, proj, proj, proj, conv_w, pad(a_log), pad(dt_bias), norm_w.reshape(1, DK))


def _rope(r, cos, sin_lo, sin_hi):
    return r * cos + pltpu.roll(r, LANES - ROPE // 2, 1) * sin_lo + pltpu.roll(r, ROPE // 2, 1) * sin_hi


def _mlaprep_kernel(qt_ref, ckv_ref, kr_ref, pos_ref, inv_ref, kvw_ref, qw_ref,
                    kwn_ref, kwr_ref, wuk_ref, wuvt_ref, qo_ref, ko_ref, vo_ref):
    tm = ckv_ref.shape[1]
    half = ROPE // 2

    ang_t = _lane_tile(inv_ref[...], tm) * pos_ref[0].astype(f32)
    cos_t = jnp.cos(ang_t)
    sin_t = jnp.sin(ang_t)
    z_half = jnp.zeros((half, tm), f32)
    z_pad = jnp.zeros((LANES - ROPE, tm), f32)
    cos = jnp.concatenate([cos_t, cos_t, z_pad], axis=0).T
    sin_lo = jnp.concatenate([-sin_t, z_half, z_pad], axis=0).T
    sin_hi = jnp.concatenate([z_half, sin_t, z_pad], axis=0).T

    ckv = ckv_ref[0].astype(f32)
    ckv_n = (ckv * lax.rsqrt(jnp.mean(ckv * ckv, axis=-1, keepdims=True) + EPS) * kvw_ref[...]).astype(bf16)
    k_nope = _dot(ckv_n, wuk_ref[...])
    vo_ref[0] = _dot_nt(wuvt_ref[...], ckv_n).astype(vo_ref.dtype)

    kr = kr_ref[0].astype(f32)
    kr_ss = jnp.sum(kr * kr, axis=-1, keepdims=True)
    kr_rot = _rope(kr * kwr_ref[...], cos, sin_lo, sin_hi)
    for h in range(H):
        kn = k_nope[:, h * NOPE:(h + 1) * NOPE]
        rk = lax.rsqrt((jnp.sum(kn * kn, axis=-1, keepdims=True) + kr_ss) * (1.0 / QK_DIM) + EPS)
        ko_ref[0, :, h * QPAD:h * QPAD + NOPE] = (kn * rk * kwn_ref[...]).astype(ko_ref.dtype)
        ko_ref[0, :, h * QPAD + NOPE:(h + 1) * QPAD] = (kr_rot * rk).astype(ko_ref.dtype)

    qw_t = _lane_tile(qw_ref[...], tm)
    for h in range(H):
        qh = qt_ref[0, h * QK_DIM:(h + 1) * QK_DIM, :].astype(f32)
        rq = lax.rsqrt(jnp.sum(qh * qh, axis=0, keepdims=True) * (1.0 / QK_DIM) + EPS) * (QK_DIM ** -0.5 * LOG2E)
        qn = qh * rq * qw_t
        x1 = qn[NOPE:NOPE + half]
        x2 = qn[NOPE + half:QK_DIM]
        r0 = h * QPAD
        qo_ref[0, r0:r0 + NOPE, :] = qn[:NOPE].astype(qo_ref.dtype)
        qo_ref[0, r0 + NOPE:r0 + NOPE + half, :] = (x1 * cos_t - x2 * sin_t).astype(qo_ref.dtype)
        qo_ref[0, r0 + NOPE + half:r0 + QK_DIM, :] = (x2 * cos_t + x1 * sin_t).astype(qo_ref.dtype)
        qo_ref[0, r0 + QK_DIM:r0 + QPAD, :] = jnp.zeros((QPAD - QK_DIM, tm), qo_ref.dtype)


def _mlaprep(proj, q_t, positions, kv_norm_w, w_uk, w_uv, q_norm_w, k_norm_w):
    B, S, _ = proj.shape
    tm = TM_PREP
    half = ROPE // 2
    inv = ROPE_THETA ** (-jnp.arange(half, dtype=f32) / half)
    inv_c = jnp.broadcast_to(inv.reshape(half, 1), (half, LANES))
    qw = jnp.broadcast_to(q_norm_w.reshape(QK_DIM, 1), (QK_DIM, LANES))
    kwn = k_norm_w[:NOPE].reshape(1, NOPE)
    kwr = jnp.pad(k_norm_w[NOPE:], (0, LANES - ROPE)).reshape(1, LANES)
    col = lambda idx, width: pl.BlockSpec((1, tm, width), lambda b, i: (b, i, idx))
    const = lambda shape: pl.BlockSpec(shape, lambda b, i: (0,) * len(shape))
    return pl.pallas_call(
        _mlaprep_kernel,
        grid=(B, S // tm),
        in_specs=[pl.BlockSpec((1, H * QK_DIM, tm), lambda b, i: (b, 0, i)),
                  col(P_CKV // KV_RANK, KV_RANK), col(P_KR // LANES, LANES),
                  pl.BlockSpec((1, 1, tm), lambda b, i: (b, 0, i)),
                  const((half, LANES)), const((1, KV_RANK)), const((QK_DIM, LANES)),
                  const((1, NOPE)), const((1, LANES)),
                  const((KV_RANK, H * NOPE)), const((H * DV, KV_RANK))],
        out_specs=[pl.BlockSpec((1, H * QPAD, tm), lambda b, i: (b, 0, i)),
                   pl.BlockSpec((1, tm, H * QPAD), lambda b, i: (b, i, 0)),
                   pl.BlockSpec((1, H * DV, tm), lambda b, i: (b, 0, i))],
        out_shape=[jax.ShapeDtypeStruct((B, H * QPAD, S), bf16),
                   jax.ShapeDtypeStruct((B, S, H * QPAD), bf16),
                   jax.ShapeDtypeStruct((B, H * DV, S), bf16)],
        compiler_params=_cparams(2),
        name="mlaprep",
    )(q_t, proj, proj, positions.reshape(B, 1, S), inv_c,
      kv_norm_w.reshape(1, KV_RANK), qw, kwn, kwr, w_uk.astype(bf16), w_uv.T.astype(bf16))


def _attn_kernel(q_ref, k_ref, v_ref, o_ref, m_ref, acc_ref, s_ref):
    G = ATT_G
    i = pl.program_id(2)
    m_ref[...] = jnp.full_like(m_ref, -jnp.inf)
    acc_ref[...] = jnp.zeros_like(acc_ref)
    last = ((i + 1) * TQ - 1) // TK

    def qk(j, slot):
        start = pl.multiple_of(j * TK, TK)
        for g in range(G):
            s_ref[slot, g] = _dot(k_ref[0, pl.ds(start, TK), g * QPAD:(g + 1) * QPAD],
                                  q_ref[0, g * QPAD:(g + 1) * QPAD, :])

    def softmax_pv(j, slot, masked):
        start = pl.multiple_of(j * TK, TK)
        if masked:
            kc = (lax.broadcasted_iota(jnp.int32, (TK, TQ), 0) + j * TK) // CHUNK
            qc = (lax.broadcasted_iota(jnp.int32, (TK, TQ), 1) + i * TQ) // CHUNK
            visible = qc >= kc
        probs, alphas = [], []
        for g in range(G):
            s = s_ref[slot, g]
            if masked:
                s = jnp.where(visible, s, -jnp.inf)
            m_prev = m_ref[g]
            m_new = jnp.maximum(m_prev, jnp.max(s, axis=0, keepdims=True))
            alphas.append(jnp.exp2(m_prev - m_new))
            probs.append(jnp.exp2(s - m_new).astype(bf16))
            m_ref[g] = m_new
        ones = jnp.ones((DV, TK), bf16)
        for g in range(G):
            v1 = jnp.concatenate([v_ref[0, g * DV:(g + 1) * DV, pl.ds(start, TK)], ones], axis=0)
            acc_ref[g] = alphas[g] * acc_ref[g] + _dot(v1, probs[g])

    qk(0, 0)

    def body(p, carry):
        qk(2 * p + 1, 1)
        softmax_pv(2 * p, 0, False)
        qk(2 * p + 2, 0)
        softmax_pv(2 * p + 1, 1, False)
        return carry

    lax.fori_loop(0, last // 2, body, 0)

    @pl.when(last % 2 == 0)
    def _():
        softmax_pv(last, 0, True)

    @pl.when(last % 2 == 1)
    def _():
        qk(last, 1)
        softmax_pv(last - 1, 0, False)
        softmax_pv(last, 1, True)

    for g in range(G):
        acc = acc_ref[g]
        o_ref[0, :, g * DV:(g + 1) * DV] = (acc[:DV] / acc[DV:]).T.astype(o_ref.dtype)


def _attn(q_t, kp, v_t):
    B, S, _ = kp.shape
    G = ATT_G
    return pl.pallas_call(
        _attn_kernel,
        grid=(B, H // G, S // TQ),
        in_specs=[pl.BlockSpec((1, G * QPAD, TQ), lambda b, h, i: (b, h, i)),
                  pl.BlockSpec((1, S, G * QPAD), lambda b, h, i: (b, 0, h)),
                  pl.BlockSpec((1, G * DV, S), lambda b, h, i: (b, h, 0))],
        out_specs=pl.BlockSpec((1, TQ, G * DV), lambda b, h, i: (b, i, h)),
        out_shape=jax.ShapeDtypeStruct((B, S, H * DV), bf16),
        scratch_shapes=[pltpu.VMEM((G, 1, TQ), f32), pltpu.VMEM((G, 2 * DV, TQ), f32),
                        pltpu.VMEM((2, G, TK, TQ), f32)],
        compiler_params=_cparams(3),
        name="attn",
    )(q_t, kp, v_t)


def _merge_kernel(x_ref, oa_ref, ob_ref, gd_ref, gm_ref, mod_ref, wa_ref, wb_ref, wo_ref, o_ref):
    y_a = _dot(oa_ref[0], wa_ref[...])
    y_b = _dot(ob_ref[0], wb_ref[...])
    mix = _sigmoid(gd_ref[0].astype(f32)) * y_a + _sigmoid(gm_ref[0].astype(f32)) * y_b
    o_ref[0] = x_ref[0] + mod_ref[0, 2:3, :] * _dot(mix.astype(bf16), wo_ref[...])


def _merge(x, o_a, o_b, proj, mod3, w_a, w_b, w_o):
    B, S, _ = x.shape
    tm = TM_MERGE
    row = lambda idx: pl.BlockSpec((1, tm, D), lambda b, i: (b, i, idx))
    wspec = pl.BlockSpec((D, D), lambda b, i: (0, 0))
    return pl.pallas_call(
        _merge_kernel,
        grid=(B, S // tm),
        in_specs=[row(0), row(0), row(0), row(P_GDN // D), row(P_GMLA // D),
                  pl.BlockSpec((1, 6, D), lambda b, i: (b, 0, 0)), wspec, wspec, wspec],
        out_specs=row(0),
        out_shape=jax.ShapeDtypeStruct((B, S, D), f32),
        compiler_params=_cparams(2),
        name="merge",
    )(x, o_a, o_b, proj, proj, mod3, w_a.astype(bf16), w_b.astype(bf16), w_o.astype(bf16))


def _ffn_kernel(x_ref, halo_ref, mod_ref, nw_ref, wa_ref, wv_ref, cw_ref, cb_ref, wd_ref, o_ref,
                h_ref, a_ref, acc_ref):
    tm = x_ref.shape[1]
    i = pl.program_id(1)
    j = pl.program_id(2)

    @pl.when(j == 0)
    def _():
        shift, scale = mod_ref[0, 3:4, :], mod_ref[0, 4:5, :]
        hh = _modulated_norm(halo_ref[0], nw_ref[...], shift, scale)
        h_ref[0:SUBLANES, :] = jnp.where(i > 0, hh, 0.0).astype(bf16)
        h_ref[SUBLANES:, :] = _modulated_norm(x_ref[0], nw_ref[...], shift, scale).astype(bf16)
        acc_ref[...] = jnp.zeros_like(acc_ref)

    a_ref[...] = _dot(h_ref[...], wa_ref[...])
    v = _dot(h_ref[SUBLANES:, :], wv_ref[...])
    a = cb_ref[...] + cw_ref[2:3, :] * a_ref[8:8 + tm, :]
    for t in range(1, FFN_CONV):
        a = a + cw_ref[2 - t:3 - t, :] * a_ref[8 - t:8 - t + tm, :]
    gelu = 0.5 * a * (1.0 + lax.erf(a * (2.0 ** -0.5)))
    acc_ref[...] += _dot((gelu * v).astype(bf16), wd_ref[...])

    @pl.when(j == pl.num_programs(2) - 1)
    def _():
        o_ref[0] = x_ref[0] + mod_ref[0, 5:6, :] * acc_ref[...]


def _ffn(x, mod3, norm_w, w_up, conv_w, conv_b, w_down):
    B, S, _ = x.shape
    tm, tf = TM_FFN, TF_FFN
    nf = D_FF // tf
    hb = tm // SUBLANES
    return pl.pallas_call(
        _ffn_kernel,
        grid=(B, S // tm, nf),
        in_specs=[pl.BlockSpec((1, tm, D), lambda b, i, j: (b, i, 0)),
                  pl.BlockSpec((1, SUBLANES, D), lambda b, i, j: (b, jnp.maximum(i * hb - 1, 0), 0)),
                  pl.BlockSpec((1, 6, D), lambda b, i, j: (b, 0, 0)),
                  pl.BlockSpec((1, D), lambda b, i, j: (0, 0)),
                  pl.BlockSpec((D, tf), lambda b, i, j: (0, j)),
                  pl.BlockSpec((D, tf), lambda b, i, j: (0, nf + j)),
                  pl.BlockSpec((FFN_CONV, tf), lambda b, i, j: (0, j)),
                  pl.BlockSpec((1, tf), lambda b, i, j: (0, j)),
                  pl.BlockSpec((tf, D), lambda b, i, j: (j, 0))],
        out_specs=pl.BlockSpec((1, tm, D), lambda b, i, j: (b, i, 0)),
        out_shape=jax.ShapeDtypeStruct((B, S, D), f32),
        scratch_shapes=[pltpu.VMEM((tm + SUBLANES, D), bf16),
                        pltpu.VMEM((tm + SUBLANES, tf), f32),
                        pltpu.VMEM((tm, D), f32)],
        compiler_params=_cparams(3),
        name="ffn",
    )(x, x, mod3, norm_w, w_up, w_up, conv_w, conv_b.reshape(1, D_FF), w_down)


def kernel(x, c, positions, w_ada, b_ada, norm1_w, w_in, dn_conv_w, dn_a_log, dn_dt_bias, dn_norm_w,
           mla_kv_norm_w, mla_w_uk, mla_w_uv, mla_q_norm_w, mla_k_norm_w, w_out_dn, w_out_mla, w_o,
           norm2_w, ffn_w_up, ffn_conv_w, ffn_conv_b, ffn_w_down):
    B = x.shape[0]
    for l in range(w_ada.shape[0]):
        mod3 = _mod(c, w_ada[l], b_ada[l]).reshape(B, 6, D)
        w_p, w_qt = _pack_w_in(w_in[l])
        proj, q_t = _inproj(x, mod3, norm1_w[l].reshape(1, D), w_p, w_qt)
        o_a = _deltanet(proj, dn_conv_w[l], dn_a_log[l], dn_dt_bias[l], dn_norm_w[l])
        qp_t, kp, v_t = _mlaprep(proj, q_t, positions, mla_kv_norm_w[l], mla_w_uk[l], mla_w_uv[l],
                                 mla_q_norm_w[l], mla_k_norm_w[l])
        o_b = _attn(qp_t, kp, v_t)
        x = _merge(x, o_a, o_b, proj, mod3, w_out_dn[l], w_out_mla[l], w_o[l])
        x = _ffn(x, mod3, norm2_w[l].reshape(1, D), ffn_w_up[l].astype(bf16), ffn_conv_w[l],
                 ffn_conv_b[l], ffn_w_down[l].astype(bf16))
    return x
```

```python
import jax
import jax.numpy as jnp
import numpy as np
from jax import lax
from jax.experimental import pallas as pl
from jax.experimental.pallas import tpu as pltpu

f32 = jnp.float32
bf16 = jnp.bfloat16

D = 1024
EPS = 1e-6
CHUNK = 64
H = 8
DK = 128
NOPE, ROPE, DV = 128, 64, 128
QK_DIM = NOPE + ROPE
KV_RANK = 256
ROPE_THETA = 10000.0
D_FF = 2816
DN_CONV = 4
FFN_CONV = 3
LOG2E = 1.4426950408889634

LANES = 128
SUBLANES = 8
VMEM_LIMIT = 56 * 1024 * 1024

QPAD = 2 * LANES
P_DNQKV = 0
P_Z = 3 * D
P_GDN = 4 * D
P_GMLA = 5 * D
P_CKV = 6 * D
P_KR = P_CKV + KV_RANK
P_AB = P_KR + LANES
NP = P_AB + LANES

DN_C = 128
DN_NB = 2
TM_IN, TN_IN = 1024, 1664
TM_PREP = 512
TQ = 512
TK = 512
ATT_G = 4
TM_MERGE = 1024
TM_FFN, TF_FFN = 1024, 1408


def _cparams(n_axes):
    return pltpu.CompilerParams(dimension_semantics=("arbitrary",) * n_axes,
                                vmem_limit_bytes=VMEM_LIMIT)


def _sigmoid(x):
    return 0.5 * (1.0 + jnp.tanh(0.5 * x))


def _silu(x):
    return x * _sigmoid(x)


def _dot(a, b):
    return jnp.dot(a, b, preferred_element_type=f32)


def _dot_nt(a, b):
    return lax.dot_general(a, b, (((1,), (1,)), ((), ())), preferred_element_type=f32)


def _dot_tn(a, b):
    return lax.dot_general(a, b, (((0,), (0,)), ((), ())), preferred_element_type=f32)


def _lane_tile(t, width):
    return jnp.concatenate([t] * (width // LANES), axis=1)


def _mod_kernel(c_ref, w_ref, b_ref, o_ref):
    a = _silu(c_ref[...]).astype(bf16)
    o_ref[...] = _dot(a, w_ref[...].astype(bf16)) + b_ref[...]


def _mod(c, w_ada, b_ada):
    B = c.shape[0]
    n = w_ada.shape[1]
    return pl.pallas_call(
        _mod_kernel,
        grid=(n // D,),
        in_specs=[pl.BlockSpec((B, D), lambda j: (0, 0)),
                  pl.BlockSpec((D, D), lambda j: (0, j)),
                  pl.BlockSpec((1, D), lambda j: (0, j))],
        out_specs=pl.BlockSpec((B, D), lambda j: (0, j)),
        out_shape=jax.ShapeDtypeStruct((B, n), f32),
        compiler_params=_cparams(1),
        name="mod",
    )(c, w_ada, b_ada.reshape(1, n))


def _pack_w_in(w_in):
    o = np.cumsum((0, D, D, D, D, H, H, H * QK_DIM, KV_RANK, ROPE, D, D))
    dn_qkv = w_in[:, o[0]:o[3]]
    dn_z = w_in[:, o[3]:o[4]]
    ab = w_in[:, o[4]:o[6]]
    mq = w_in[:, o[6]:o[7]]
    ckv = w_in[:, o[7]:o[8]]
    kr = w_in[:, o[8]:o[9]]
    g_dn = w_in[:, o[9]:o[10]]
    g_mla = w_in[:, o[10]:o[11]]
    kr = jnp.pad(kr, ((0, 0), (0, LANES - ROPE)))
    ab = jnp.pad(ab, ((0, 0), (0, LANES - 2 * H)))
    w_p = jnp.concatenate([dn_qkv, dn_z, g_dn, g_mla, ckv, kr, ab], axis=1).astype(bf16)
    return w_p, mq.T.astype(bf16)


def _modulated_norm(x, nw, shift, scale):
    r = lax.rsqrt(jnp.mean(x * x, axis=-1, keepdims=True) + EPS)
    return (x * r * nw) * (1.0 + scale) + shift


def _inproj_kernel(x_ref, mod_ref, nw_ref, w_ref, wq_ref, o_ref, qt_ref, h_ref):
    @pl.when(pl.program_id(2) == 0)
    def _():
        h = _modulated_norm(x_ref[0], nw_ref[...], mod_ref[0, 0:1, :], mod_ref[0, 1:2, :])
        h_ref[...] = h.astype(bf16)
        qt_ref[0] = _dot_nt(wq_ref[...], h_ref[...]).astype(qt_ref.dtype)

    o_ref[0] = _dot(h_ref[...], w_ref[...]).astype(o_ref.dtype)


def _inproj(x, mod3, norm_w, w_p, w_qt):
    B, S, _ = x.shape
    return pl.pallas_call(
        _inproj_kernel,
        grid=(B, S // TM_IN, NP // TN_IN),
        in_specs=[pl.BlockSpec((1, TM_IN, D), lambda b, i, j: (b, i, 0)),
                  pl.BlockSpec((1, 6, D), lambda b, i, j: (b, 0, 0)),
                  pl.BlockSpec((1, D), lambda b, i, j: (0, 0)),
                  pl.BlockSpec((D, TN_IN), lambda b, i, j: (0, j)),
                  pl.BlockSpec((H * QK_DIM, D), lambda b, i, j: (0, 0))],
        out_specs=[pl.BlockSpec((1, TM_IN, TN_IN), lambda b, i, j: (b, i, j)),
                   pl.BlockSpec((1, H * QK_DIM, TM_IN), lambda b, i, j: (b, 0, i))],
        out_shape=[jax.ShapeDtypeStruct((B, S, NP), bf16),
                   jax.ShapeDtypeStruct((B, H * QK_DIM, S), bf16)],
        scratch_shapes=[pltpu.VMEM((TM_IN, D), bf16)],
        compiler_params=_cparams(3),
        name="inproj",
    )(x, mod3, norm_w, w_p, w_qt)


def _cumsum_rows(x):
    n = x.shape[0]
    row = lax.broadcasted_iota(jnp.int32, x.shape, 0)
    s = 1
    while s < n:
        x = x + jnp.where(row >= s, pltpu.roll(x, s, 0), 0.0)
        s *= 2
    return x


def _pair_cols(t, ca, cb, rows):
    return jnp.concatenate([jnp.broadcast_to(t[:, ca:ca + 1], (rows, LANES)),
                            jnp.broadcast_to(t[:, cb:cb + 1], (rows, LANES))], axis=1)


def _block_diag_rows(x):
    z = jnp.zeros((x.shape[0], LANES), x.dtype)
    return jnp.concatenate([jnp.concatenate([x[:, :LANES], z], axis=1),
                            jnp.concatenate([z, x[:, LANES:]], axis=1)], axis=0)


def _doubling_masks(n, block):
    r = lax.broadcasted_iota(jnp.int32, (n, n), 0)
    c = lax.broadcasted_iota(jnp.int32, (n, n), 1)
    rx = r ^ c
    eye = rx == 0
    pair_mask = (rx == 1) & ((r & 1) == 1)
    levels = []
    s = 2
    while s < block:
        levels.append((rx >= s) & (rx < 2 * s) & ((r & s) != 0))
        s *= 2
    return eye, pair_mask, levels


def _deltanet_kernel(q_ref, k_ref, v_ref, z_ref, ab_ref, cw_ref, alog_ref, dtb_ref, nw_ref,
                     o_ref, xpad_ref, state_ref):
    C = DN_C
    P2 = 2 * LANES
    c = pl.program_id(1)

    @pl.when(c == 0)
    def _():
        xpad_ref[:, 0:SUBLANES, :] = jnp.zeros((DN_NB, SUBLANES, 3 * D), f32)
        state_ref[...] = jnp.zeros_like(state_ref)

    @pl.when(c > 0)
    def _():
        xpad_ref[:, 0:SUBLANES, :] = xpad_ref[:, C:C + SUBLANES, :]

    xpad_ref[:, SUBLANES:SUBLANES + C, 0:D] = q_ref[...].astype(f32)
    xpad_ref[:, SUBLANES:SUBLANES + C, D:2 * D] = k_ref[...].astype(f32)
    xpad_ref[:, SUBLANES:SUBLANES + C, 2 * D:3 * D] = v_ref[...].astype(f32)

    row = lax.broadcasted_iota(jnp.int32, (C, P2), 0)
    col = lax.broadcasted_iota(jnp.int32, (C, P2), 1) & (LANES - 1)
    tri = row >= col
    strict = row > col
    masks = _doubling_masks(P2, C)
    r2 = lax.broadcasted_iota(jnp.int32, (P2, P2), 0)
    c2 = lax.broadcasted_iota(jnp.int32, (P2, P2), 1)
    same_head = (r2 < LANES) == (c2 < LANES)

    def conv_silu(bi, col0):
        sl = slice(col0, col0 + P2)
        xa = xpad_ref[bi, :, sl]
        y = cw_ref[3:4, sl] * xa[SUBLANES:]
        for j in range(1, DN_CONV):
            y = y + cw_ref[3 - j:4 - j, sl] * pltpu.roll(xa, j, 0)[SUBLANES:]
        return _silu(y)

    def l2n(t):
        t2 = t * t
        ra = lax.rsqrt(jnp.sum(t2[:, :LANES], axis=-1, keepdims=True) + EPS)
        rb = lax.rsqrt(jnp.sum(t2[:, LANES:], axis=-1, keepdims=True) + EPS)
        return jnp.concatenate([t[:, :LANES] * ra, t[:, LANES:] * rb], axis=1)

    n_pairs = H // 2
    chains = [(bi, p) for bi in range(DN_NB) for p in range(n_pairs)]
    decays = []
    for bi in range(DN_NB):
        ab = ab_ref[bi].astype(f32)
        x = ab + dtb_ref[...]
        softplus = jnp.maximum(x, 0.0) + jnp.log1p(jnp.exp(-jnp.abs(x)))
        gc = _cumsum_rows(-jnp.exp(alog_ref[...]) * softplus)
        decays.append((gc, gc.T, _sigmoid(ab)))
    pr = []
    for bi, p in chains:
        gc, gc_t, beta_t = decays[bi]
        ha, hb = 2 * p, 2 * p + 1
        q = l2n(conv_silu(bi, p * P2)) * (DK ** -0.5)
        k = l2n(conv_silu(bi, D + p * P2))
        v = conv_silu(bi, 2 * D + p * P2)
        gcb = _pair_cols(gc, ha, hb, C)
        betab = _pair_cols(beta_t, H + ha, H + hb, C)
        gcr = jnp.concatenate([jnp.broadcast_to(gc_t[ha:ha + 1, :], (C, LANES)),
                               jnp.broadcast_to(gc_t[hb:hb + 1, :], (C, LANES))], axis=1)
        decay = jnp.exp(jnp.where(tri, gcb - gcr, -1e30))
        eg = jnp.exp(gcb)
        gl = jnp.concatenate([jnp.broadcast_to(gc[C - 1:C, ha:ha + 1], (C, LANES)),
                              jnp.broadcast_to(gc[C - 1:C, hb:hb + 1], (C, LANES))], axis=1)
        kb = k * betab
        vb = v * betab
        gram = _dot_nt(jnp.concatenate([kb, q], axis=0).astype(bf16),
                       _block_diag_rows(k.astype(bf16)))
        kbe = kb * eg
        rhs = jnp.concatenate([jnp.concatenate([vb[:, :LANES], kbe[:, :LANES]], axis=1),
                               jnp.concatenate([vb[:, LANES:], kbe[:, LANES:]], axis=1)], axis=0).astype(bf16)
        pr.append(dict(
            a16=_block_diag_rows(jnp.where(strict, gram[:C] * decay, 0.0).astype(bf16)),
            intra=jnp.where(tri, gram[C:] * decay, 0.0).astype(bf16),
            rhs=rhs, qe=q * eg, k_dec=(k * jnp.exp(gl - gcb)).astype(bf16),
            dec_rows=jnp.concatenate(
                [jnp.broadcast_to(jnp.exp(gc[C - 1:C, ha:ha + 1]), (LANES, P2)),
                 jnp.broadcast_to(jnp.exp(gc[C - 1:C, hb:hb + 1]), (LANES, P2))], axis=0)))

    eye, pair_mask, level_masks = masks
    level_sel = [jnp.where(m, 1.0, 0.0).astype(bf16) for m in level_masks]
    eye16 = jnp.where(eye, 1.0, 0.0).astype(bf16)
    pair_sel = jnp.where(pair_mask, 1.0, 0.0).astype(bf16)
    a16s = [d["a16"] for d in pr]
    ts = [eye16 - a16 * pair_sel for a16 in a16s]
    for sel in level_sel:
        inner = [_dot(a16 * sel, t).astype(bf16) for a16, t in zip(a16s, ts)]
        ts = [t - _dot(t, x).astype(bf16) for t, x in zip(ts, inner)]

    sols = [_dot(t, d["rhs"]) for t, d in zip(ts, pr)]
    us = [jnp.concatenate([s[:C, :LANES], s[C:, :LANES]], axis=1) for s in sols]
    ws = [jnp.concatenate([s[:C, LANES:], s[C:, LANES:]], axis=1) for s in sols]
    s2s = [state_ref[n] for n in range(len(chains))]
    wqs = [_dot(jnp.concatenate([w, d["qe"]], axis=0).astype(bf16), s2.astype(bf16))
           for w, d, s2 in zip(ws, pr, s2s)]
    v_news = [u - wq[:C] for u, wq in zip(us, wqs)]
    vn16s = [vn.astype(bf16) for vn in v_news]
    outs = [wq[C:] + _dot(d["intra"], _block_diag_rows(vn16)) for wq, d, vn16 in zip(wqs, pr, vn16s)]
    kvs = [_dot_tn(d["k_dec"], vn16) for d, vn16 in zip(pr, vn16s)]
    for n, (bi, p) in enumerate(chains):
        state_ref[n] = s2s[n] * pr[n]["dec_rows"] + jnp.where(same_head, kvs[n], 0.0)
        o = outs[n]
        o2 = o * o
        ra = lax.rsqrt(jnp.mean(o2[:, :LANES], axis=-1, keepdims=True) + EPS)
        rb = lax.rsqrt(jnp.mean(o2[:, LANES:], axis=-1, keepdims=True) + EPS)
        on = jnp.concatenate([o[:, :LANES] * ra * nw_ref[...], o[:, LANES:] * rb * nw_ref[...]], axis=1)
        zz = z_ref[bi, :, p * P2:(p + 1) * P2].astype(f32)
        o_ref[bi, :, p * P2:(p + 1) * P2] = (on * _silu(zz)).astype(o_ref.dtype)


def _deltanet(proj, conv_w, a_log, dt_bias, norm_w):
    B, S, _ = proj.shape
    C = DN_C
    pad = lambda t: jnp.pad(t.reshape(1, H), ((0, 0), (0, LANES - H)))
    col = lambda idx, width: pl.BlockSpec((DN_NB, C, width), lambda b, c: (b, c, idx))
    const = lambda shape: pl.BlockSpec(shape, lambda b, c: (0,) * len(shape))
    return pl.pallas_call(
        _deltanet_kernel,
        grid=(B // DN_NB, S // C),
        in_specs=[col(0, D), col(1, D), col(2, D), col(P_Z // D, D), col(P_AB // LANES, LANES),
                  const((DN_CONV, 3 * D)), const((1, LANES)), const((1, LANES)), const((1, DK))],
        out_specs=pl.BlockSpec((DN_NB, C, D), lambda b, c: (b, c, 0)),
        out_shape=jax.ShapeDtypeStruct((B, S, D), bf16),
        scratch_shapes=[pltpu.VMEM((DN_NB, C + SUBLANES, 3 * D), f32),
                        pltpu.VMEM((DN_NB * (H // 2), 2 * LANES, 2 * LANES), f32)],
        compiler_params=_cparams(2),
        name="deltanet",
    )(proj, proj, proj, proj, proj, conv_w, pad(a_log), pad(dt_bias), norm_w.reshape(1, DK))


def _rope(r, cos, sin_lo, sin_hi):
    return r * cos + pltpu.roll(r, LANES - ROPE // 2, 1) * sin_lo + pltpu.roll(r, ROPE // 2, 1) * sin_hi


def _mlaprep_kernel(qt_ref, ckv_ref, kr_ref, pos_ref, inv_ref, kvw_ref, qw_ref,
                    kwn_ref, kwr_ref, wuk_ref, wuvt_ref, qo_ref, ko_ref, vo_ref):
    tm = ckv_ref.shape[1]
    half = ROPE // 2

    ang_t = _lane_tile(inv_ref[...], tm) * pos_ref[0].astype(f32)
    cos_t = jnp.cos(ang_t)
    sin_t = jnp.sin(ang_t)
    z_half = jnp.zeros((half, tm), f32)
    z_pad = jnp.zeros((LANES - ROPE, tm), f32)
    cos = jnp.concatenate([cos_t, cos_t, z_pad], axis=0).T
    sin_lo = jnp.concatenate([-sin_t, z_half, z_pad], axis=0).T
    sin_hi = jnp.concatenate([z_half, sin_t, z_pad], axis=0).T

    ckv = ckv_ref[0].astype(f32)
    ckv_n = (ckv * lax.rsqrt(jnp.mean(ckv * ckv, axis=-1, keepdims=True) + EPS) * kvw_ref[...]).astype(bf16)
    k_nope = _dot(ckv_n, wuk_ref[...])
    vo_ref[0] = _dot_nt(wuvt_ref[...], ckv_n).astype(vo_ref.dtype)

    kr = kr_ref[0].astype(f32)
    kr_ss = jnp.sum(kr * kr, axis=-1, keepdims=True)
    kr_rot = _rope(kr * kwr_ref[...], cos, sin_lo, sin_hi)
    for h in range(H):
        kn = k_nope[:, h * NOPE:(h + 1) * NOPE]
        rk = lax.rsqrt((jnp.sum(kn * kn, axis=-1, keepdims=True) + kr_ss) * (1.0 / QK_DIM) + EPS)
        ko_ref[0, :, h * QPAD:h * QPAD + NOPE] = (kn * rk * kwn_ref[...]).astype(ko_ref.dtype)
        ko_ref[0, :, h * QPAD + NOPE:(h + 1) * QPAD] = (kr_rot * rk).astype(ko_ref.dtype)

    qw_t = _lane_tile(qw_ref[...], tm)
    for h in range(H):
        qh = qt_ref[0, h * QK_DIM:(h + 1) * QK_DIM, :].astype(f32)
        rq = lax.rsqrt(jnp.sum(qh * qh, axis=0, keepdims=True) * (1.0 / QK_DIM) + EPS) * (QK_DIM ** -0.5 * LOG2E)
        qn = qh * rq * qw_t
        x1 = qn[NOPE:NOPE + half]
        x2 = qn[NOPE + half:QK_DIM]
        r0 = h * QPAD
        qo_ref[0, r0:r0 + NOPE, :] = qn[:NOPE].astype(qo_ref.dtype)
        qo_ref[0, r0 + NOPE:r0 + NOPE + half, :] = (x1 * cos_t - x2 * sin_t).astype(qo_ref.dtype)
        qo_ref[0, r0 + NOPE + half:r0 + QK_DIM, :] = (x2 * cos_t + x1 * sin_t).astype(qo_ref.dtype)
        qo_ref[0, r0 + QK_DIM:r0 + QPAD, :] = jnp.zeros((QPAD - QK_DIM, tm), qo_ref.dtype)


def _mlaprep(proj, q_t, positions, kv_norm_w, w_uk, w_uv, q_norm_w, k_norm_w):
    B, S, _ = proj.shape
    tm = TM_PREP
    half = ROPE // 2
    inv = ROPE_THETA ** (-jnp.arange(half, dtype=f32) / half)
    inv_c = jnp.broadcast_to(inv.reshape(half, 1), (half, LANES))
    qw = jnp.broadcast_to(q_norm_w.reshape(QK_DIM, 1), (QK_DIM, LANES))
    kwn = k_norm_w[:NOPE].reshape(1, NOPE)
    kwr = jnp.pad(k_norm_w[NOPE:], (0, LANES - ROPE)).reshape(1, LANES)
    col = lambda idx, width: pl.BlockSpec((1, tm, width), lambda b, i: (b, i, idx))
    const = lambda shape: pl.BlockSpec(shape, lambda b, i: (0,) * len(shape))
    return pl.pallas_call(
        _mlaprep_kernel,
        grid=(B, S // tm),
        in_specs=[pl.BlockSpec((1, H * QK_DIM, tm), lambda b, i: (b, 0, i)),
                  col(P_CKV // KV_RANK, KV_RANK), col(P_KR // LANES, LANES),
                  pl.BlockSpec((1, 1, tm), lambda b, i: (b, 0, i)),
                  const((half, LANES)), const((1, KV_RANK)), const((QK_DIM, LANES)),
                  const((1, NOPE)), const((1, LANES)),
                  const((KV_RANK, H * NOPE)), const((H * DV, KV_RANK))],
        out_specs=[pl.BlockSpec((1, H * QPAD, tm), lambda b, i: (b, 0, i)),
                   pl.BlockSpec((1, tm, H * QPAD), lambda b, i: (b, i, 0)),
                   pl.BlockSpec((1, H * DV, tm), lambda b, i: (b, 0, i))],
        out_shape=[jax.ShapeDtypeStruct((B, H * QPAD, S), bf16),
                   jax.ShapeDtypeStruct((B, S, H * QPAD), bf16),
                   jax.ShapeDtypeStruct((B, H * DV, S), bf16)],
        compiler_params=_cparams(2),
        name="mlaprep",
    )(q_t, proj, proj, positions.reshape(B, 1, S), inv_c,
      kv_norm_w.reshape(1, KV_RANK), qw, kwn, kwr, w_uk.astype(bf16), w_uv.T.astype(bf16))


def _attn_kernel(q_ref, k_ref, v_ref, o_ref, m_ref, acc_ref, s_ref):
    G = ATT_G
    i = pl.program_id(2)
    m_ref[...] = jnp.full_like(m_ref, -jnp.inf)
    acc_ref[...] = jnp.zeros_like(acc_ref)
    last = ((i + 1) * TQ - 1) // TK

    def qk(j, slot):
        start = pl.multiple_of(j * TK, TK)
        for g in range(G):
            s_ref[slot, g] = _dot(k_ref[0, pl.ds(start, TK), g * QPAD:(g + 1) * QPAD],
                                  q_ref[0, g * QPAD:(g + 1) * QPAD, :])

    def softmax_pv(j, slot, masked):
        start = pl.multiple_of(j * TK, TK)
        if masked:
            kc = (lax.broadcasted_iota(jnp.int32, (TK, TQ), 0) + j * TK) // CHUNK
            qc = (lax.broadcasted_iota(jnp.int32, (TK, TQ), 1) + i * TQ) // CHUNK
            visible = qc >= kc
        probs, alphas = [], []
        for g in range(G):
            s = s_ref[slot, g]
            if masked:
                s = jnp.where(visible, s, -jnp.inf)
            m_prev = m_ref[g]
            m_new = jnp.maximum(m_prev, jnp.max(s, axis=0, keepdims=True))
            alphas.append(jnp.exp2(m_prev - m_new))
            probs.append(jnp.exp2(s - m_new).astype(bf16))
            m_ref[g] = m_new
        ones = jnp.ones((DV, TK), bf16)
        for g in range(G):
            v1 = jnp.concatenate([v_ref[0, g * DV:(g + 1) * DV, pl.ds(start, TK)], ones], axis=0)
            acc_ref[g] = alphas[g] * acc_ref[g] + _dot(v1, probs[g])

    qk(0, 0)

    def body(p, carry):
        qk(2 * p + 1, 1)
        softmax_pv(2 * p, 0, False)
        qk(2 * p + 2, 0)
        softmax_pv(2 * p + 1, 1, False)
        return carry

    lax.fori_loop(0, last // 2, body, 0)

    @pl.when(last % 2 == 0)
    def _():
        softmax_pv(last, 0, True)

    @pl.when(last % 2 == 1)
    def _():
        qk(last, 1)
        softmax_pv(last - 1, 0, False)
        softmax_pv(last, 1, True)

    for g in range(G):
        acc = acc_ref[g]
        o_ref[0, :, g * DV:(g + 1) * DV] = (acc[:DV] / acc[DV:]).T.astype(o_ref.dtype)


def _attn(q_t, kp, v_t):
    B, S, _ = kp.shape
    G = ATT_G
    return pl.pallas_call(
        _attn_kernel,
        grid=(B, H // G, S // TQ),
        in_specs=[pl.BlockSpec((1, G * QPAD, TQ), lambda b, h, i: (b, h, i)),
                  pl.BlockSpec((1, S, G * QPAD), lambda b, h, i: (b, 0, h)),
                  pl.BlockSpec((1, G * DV, S), lambda b, h, i: (b, h, 0))],
        out_specs=pl.BlockSpec((1, TQ, G * DV), lambda b, h, i: (b, i, h)),
        out_shape=jax.ShapeDtypeStruct((B, S, H * DV), bf16),
        scratch_shapes=[pltpu.VMEM((G, 1, TQ), f32), pltpu.VMEM((G, 2 * DV, TQ), f32),
                        pltpu.VMEM((2, G, TK, TQ), f32)],
        compiler_params=_cparams(3),
        name="attn",
    )(q_t, kp, v_t)


def _merge_kernel(x_ref, oa_ref, ob_ref, gd_ref, gm_ref, mod_ref, wa_ref, wb_ref, wo_ref, o_ref):
    y_a = _dot(oa_ref[0], wa_ref[...])
    y_b = _dot(ob_ref[0], wb_ref[...])
    mix = _sigmoid(gd_ref[0].astype(f32)) * y_a + _sigmoid(gm_ref[0].astype(f32)) * y_b
    o_ref[0] = x_ref[0] + mod_ref[0, 2:3, :] * _dot(mix.astype(bf16), wo_ref[...])


def _merge(x, o_a, o_b, proj, mod3, w_a, w_b, w_o):
    B, S, _ = x.shape
    tm = TM_MERGE
    row = lambda idx: pl.BlockSpec((1, tm, D), lambda b, i: (b, i, idx))
    wspec = pl.BlockSpec((D, D), lambda b, i: (0, 0))
    return pl.pallas_call(
        _merge_kernel,
        grid=(B, S // tm),
        in_specs=[row(0), row(0), row(0), row(P_GDN // D), row(P_GMLA // D),
                  pl.BlockSpec((1, 6, D), lambda b, i: (b, 0, 0)), wspec, wspec, wspec],
        out_specs=row(0),
        out_shape=jax.ShapeDtypeStruct((B, S, D), f32),
        compiler_params=_cparams(2),
        name="merge",
    )(x, o_a, o_b, proj, proj, mod3, w_a.astype(bf16), w_b.astype(bf16), w_o.astype(bf16))


def _ffn_kernel(x_ref, halo_ref, mod_ref, nw_ref, wa_ref, wv_ref, cw_ref, cb_ref, wd_ref, o_ref,
                h_ref, a_ref, acc_ref):
    tm = x_ref.shape[1]
    i = pl.program_id(1)
    j = pl.program_id(2)

    @pl.when(j == 0)
    def _():
        shift, scale = mod_ref[0, 3:4, :], mod_ref[0, 4:5, :]
        hh = _modulated_norm(halo_ref[0], nw_ref[...], shift, scale)
        h_ref[0:SUBLANES, :] = jnp.where(i > 0, hh, 0.0).astype(bf16)
        h_ref[SUBLANES:, :] = _modulated_norm(x_ref[0], nw_ref[...], shift, scale).astype(bf16)
        acc_ref[...] = jnp.zeros_like(acc_ref)

    a_ref[...] = _dot(h_ref[...], wa_ref[...])
    v = _dot(h_ref[SUBLANES:, :], wv_ref[...])
    a = cb_ref[...] + cw_ref[2:3, :] * a_ref[8:8 + tm, :]
    for t in range(1, FFN_CONV):
        a = a + cw_ref[2 - t:3 - t, :] * a_ref[8 - t:8 - t + tm, :]
    gelu = 0.5 * a * (1.0 + lax.erf(a * (2.0 ** -0.5)))
    acc_ref[...] += _dot((gelu * v).astype(bf16), wd_ref[...])

    @pl.when(j == pl.num_programs(2) - 1)
    def _():
        o_ref[0] = x_ref[0] + mod_ref[0, 5:6, :] * acc_ref[...]


def _ffn(x, mod3, norm_w, w_up, conv_w, conv_b, w_down):
    B, S, _ = x.shape
    tm, tf = TM_FFN, TF_FFN
    nf = D_FF // tf
    hb = tm // SUBLANES
    return pl.pallas_call(
        _ffn_kernel,
        grid=(B, S // tm, nf),
        in_specs=[pl.BlockSpec((1, tm, D), lambda b, i, j: (b, i, 0)),
                  pl.BlockSpec((1, SUBLANES, D), lambda b, i, j: (b, jnp.maximum(i * hb - 1, 0), 0)),
                  pl.BlockSpec((1, 6, D), lambda b, i, j: (b, 0, 0)),
                  pl.BlockSpec((1, D), lambda b, i, j: (0, 0)),
                  pl.BlockSpec((D, tf), lambda b, i, j: (0, j)),
                  pl.BlockSpec((D, tf), lambda b, i, j: (0, nf + j)),
                  pl.BlockSpec((FFN_CONV, tf), lambda b, i, j: (0, j)),
                  pl.BlockSpec((1, tf), lambda b, i, j: (0, j)),
                  pl.BlockSpec((tf, D), lambda b, i, j: (j, 0))],
        out_specs=pl.BlockSpec((1, tm, D), lambda b, i, j: (b, i, 0)),
        out_shape=jax.ShapeDtypeStruct((B, S, D), f32),
        scratch_shapes=[pltpu.VMEM((tm + SUBLANES, D), bf16),
                        pltpu.VMEM((tm + SUBLANES, tf), f32),
                        pltpu.VMEM((tm, D), f32)],
        compiler_params=_cparams(3),
        name="ffn",
    )(x, x, mod3, norm_w, w_up, w_up, conv_w, conv_b.reshape(1, D_FF), w_down)


def kernel(x, c, positions, w_ada, b_ada, norm1_w, w_in, dn_conv_w, dn_a_log, dn_dt_bias, dn_norm_w,
           mla_kv_norm_w, mla_w_uk, mla_w_uv, mla_q_norm_w, mla_k_norm_w, w_out_dn, w_out_mla, w_o,
           norm2_w, ffn_w_up, ffn_conv_w, ffn_conv_b, ffn_w_down):
    B = x.shape[0]
    for l in range(w_ada.shape[0]):
        mod3 = _mod(c, w_ada[l], b_ada[l]).reshape(B, 6, D)
        w_p, w_qt = _pack_w_in(w_in[l])
        proj, q_t = _inproj(x, mod3, norm1_w[l].reshape(1, D), w_p, w_qt)
        o_a = _deltanet(proj, dn_conv_w[l], dn_a_log[l], dn_dt_bias[l], dn_norm_w[l])
        qp_t, kp, v_t = _mlaprep(proj, q_t, positions, mla_kv_norm_w[l], mla_w_uk[l], mla_w_uv[l],
                                 mla_q_norm_w[l], mla_k_norm_w[l])
        o_b = _attn(qp_t, kp, v_t)
        x = _merge(x, o_a, o_b, proj, mod3, w_out_dn[l], w_out_mla[l], w_o[l])
        x = _ffn(x, mod3, norm2_w[l].reshape(1, D), ffn_w_up[l].astype(bf16), ffn_conv_w[l],
                 ffn_conv_b[l], ffn_w_down[l].astype(bf16))
    return x
```

```python
import jax
import jax.numpy as jnp
import numpy as np
from jax import lax
from jax.experimental import pallas as pl
from jax.experimental.pallas import tpu as pltpu

f32 = jnp.float32
bf16 = jnp.bfloat16

D = 1024
EPS = 1e-6
CHUNK = 64
H = 8
DK = 128
NOPE, ROPE, DV = 128, 64, 128
QK_DIM = NOPE + ROPE
KV_RANK = 256
ROPE_THETA = 10000.0
D_FF = 2816
DN_CONV = 4
FFN_CONV = 3
LOG2E = 1.4426950408889634

LANES = 128
SUBLANES = 8
VMEM_LIMIT = 56 * 1024 * 1024

QPAD = 2 * LANES
P_DNQKV = 0
P_Z = 3 * D
P_GDN = 4 * D
P_GMLA = 5 * D
P_CKV = 6 * D
P_KR = P_CKV + KV_RANK
P_AB = P_KR + LANES
NP = P_AB + LANES

DN_C = 128
DN_NB = 2
TM_IN, TN_IN = 1024, 1664
TM_PREP = 512
TQ = 512
TK = 512
ATT_G = 4
TM_MERGE = 1024
TM_FFN, TF_FFN = 1024, 1408


def _cparams(n_axes):
    return pltpu.CompilerParams(dimension_semantics=("arbitrary",) * n_axes,
                                vmem_limit_bytes=VMEM_LIMIT)


def _sigmoid(x):
    return 0.5 * (1.0 + jnp.tanh(0.5 * x))


def _silu(x):
    return x * _sigmoid(x)


def _dot(a, b):
    return jnp.dot(a, b, preferred_element_type=f32)


def _dot_nt(a, b):
    return lax.dot_general(a, b, (((1,), (1,)), ((), ())), preferred_element_type=f32)


def _dot_tn(a, b):
    return lax.dot_general(a, b, (((0,), (0,)), ((), ())), preferred_element_type=f32)


def _lane_tile(t, width):
    return jnp.concatenate([t] * (width // LANES), axis=1)


def _mod_kernel(c_ref, w_ref, b_ref, o_ref):
    a = _silu(c_ref[...]).astype(bf16)
    o_ref[...] = _dot(a, w_ref[...].astype(bf16)) + b_ref[...]


def _mod(c, w_ada, b_ada):
    B = c.shape[0]
    n = w_ada.shape[1]
    return pl.pallas_call(
        _mod_kernel,
        grid=(n // D,),
        in_specs=[pl.BlockSpec((B, D), lambda j: (0, 0)),
                  pl.BlockSpec((D, D), lambda j: (0, j)),
                  pl.BlockSpec((1, D), lambda j: (0, j))],
        out_specs=pl.BlockSpec((B, D), lambda j: (0, j)),
        out_shape=jax.ShapeDtypeStruct((B, n), f32),
        compiler_params=_cparams(1),
        name="mod",
    )(c, w_ada, b_ada.reshape(1, n))


def _pack_w_in(w_in):
    o = np.cumsum((0, D, D, D, D, H, H, H * QK_DIM, KV_RANK, ROPE, D, D))
    dn_qkv = w_in[:, o[0]:o[3]]
    dn_z = w_in[:, o[3]:o[4]]
    ab = w_in[:, o[4]:o[6]]
    mq = w_in[:, o[6]:o[7]]
    ckv = w_in[:, o[7]:o[8]]
    kr = w_in[:, o[8]:o[9]]
    g_dn = w_in[:, o[9]:o[10]]
    g_mla = w_in[:, o[10]:o[11]]
    kr = jnp.pad(kr, ((0, 0), (0, LANES - ROPE)))
    ab = jnp.pad(ab, ((0, 0), (0, LANES - 2 * H)))
    w_p = jnp.concatenate([dn_qkv, dn_z, g_dn, g_mla, ckv, kr, ab], axis=1).astype(bf16)
    return w_p, mq.T.astype(bf16)


def _modulated_norm(x, nw, shift, scale):
    r = lax.rsqrt(jnp.mean(x * x, axis=-1, keepdims=True) + EPS)
    return (x * r * nw) * (1.0 + scale) + shift


def _inproj_kernel(x_ref, mod_ref, nw_ref, w_ref, wq_ref, o_ref, qt_ref, h_ref):
    @pl.when(pl.program_id(2) == 0)
    def _():
        h = _modulated_norm(x_ref[0], nw_ref[...], mod_ref[0, 0:1, :], mod_ref[0, 1:2, :])
        h_ref[...] = h.astype(bf16)
        qt_ref[0] = _dot_nt(wq_ref[...], h_ref[...]).astype(qt_ref.dtype)

    o_ref[0] = _dot(h_ref[...], w_ref[...]).astype(o_ref.dtype)


def _inproj(x, mod3, norm_w, w_p, w_qt):
    B, S, _ = x.shape
    return pl.pallas_call(
        _inproj_kernel,
        grid=(B, S // TM_IN, NP // TN_IN),
        in_specs=[pl.BlockSpec((1, TM_IN, D), lambda b, i, j: (b, i, 0)),
                  pl.BlockSpec((1, 6, D), lambda b, i, j: (b, 0, 0)),
                  pl.BlockSpec((1, D), lambda b, i, j: (0, 0)),
                  pl.BlockSpec((D, TN_IN), lambda b, i, j: (0, j)),
                  pl.BlockSpec((H * QK_DIM, D), lambda b, i, j: (0, 0))],
        out_specs=[pl.BlockSpec((1, TM_IN, TN_IN), lambda b, i, j: (b, i, j)),
                   pl.BlockSpec((1, H * QK_DIM, TM_IN), lambda b, i, j: (b, 0, i))],
        out_shape=[jax.ShapeDtypeStruct((B, S, NP), bf16),
                   jax.ShapeDtypeStruct((B, H * QK_DIM, S), bf16)],
        scratch_shapes=[pltpu.VMEM((TM_IN, D), bf16)],
        compiler_params=_cparams(3),
        name="inproj",
    )(x, mod3, norm_w, w_p, w_qt)


def _cumsum_rows(x):
    n = x.shape[0]
    row = lax.broadcasted_iota(jnp.int32, x.shape, 0)
    s = 1
    while s < n:
        x = x + jnp.where(row >= s, pltpu.roll(x, s, 0), 0.0)
        s *= 2
    return x


def _pair_cols(t, ca, cb, rows):
    return jnp.concatenate([jnp.broadcast_to(t[:, ca:ca + 1], (rows, LANES)),
                            jnp.broadcast_to(t[:, cb:cb + 1], (rows, LANES))], axis=1)


def _block_diag_rows(x):
    z = jnp.zeros((x.shape[0], LANES), x.dtype)
    return jnp.concatenate([jnp.concatenate([x[:, :LANES], z], axis=1),
                            jnp.concatenate([z, x[:, LANES:]], axis=1)], axis=0)


def _doubling_masks(n, block):
    r = lax.broadcasted_iota(jnp.int32, (n, n), 0)
    c = lax.broadcasted_iota(jnp.int32, (n, n), 1)
    rx = r ^ c
    eye = rx == 0
    pair_mask = (rx == 1) & ((r & 1) == 1)
    levels = []
    s = 2
    while s < block:
        levels.append((rx >= s) & (rx < 2 * s) & ((r & s) != 0))
        s *= 2
    return eye, pair_mask, levels


def _deltanet_kernel(q_ref, k_ref, v_ref, z_ref, ab_ref, cw_ref, alog_ref, dtb_ref, nw_ref,
                     o_ref, xpad_ref, state_ref):
    C = DN_C
    P2 = 2 * LANES
    c = pl.program_id(1)

    @pl.when(c == 0)
    def _():
        xpad_ref[:, 0:SUBLANES, :] = jnp.zeros((DN_NB, SUBLANES, 3 * D), f32)
        state_ref[...] = jnp.zeros_like(state_ref)

    @pl.when(c > 0)
    def _():
        xpad_ref[:, 0:SUBLANES, :] = xpad_ref[:, C:C + SUBLANES, :]

    xpad_ref[:, SUBLANES:SUBLANES + C, 0:D] = q_ref[...].astype(f32)
    xpad_ref[:, SUBLANES:SUBLANES + C, D:2 * D] = k_ref[...].astype(f32)
    xpad_ref[:, SUBLANES:SUBLANES + C, 2 * D:3 * D] = v_ref[...].astype(f32)

    row = lax.broadcasted_iota(jnp.int32, (C, P2), 0)
    col = lax.broadcasted_iota(jnp.int32, (C, P2), 1) & (LANES - 1)
    tri = row >= col
    strict = row > col
    masks = _doubling_masks(P2, C)
    r2 = lax.broadcasted_iota(jnp.int32, (P2, P2), 0)
    c2 = lax.broadcasted_iota(jnp.int32, (P2, P2), 1)
    same_head = (r2 < LANES) == (c2 < LANES)

    def conv_silu(bi, col0):
        sl = slice(col0, col0 + P2)
        xa = xpad_ref[bi, :, sl]
        y = cw_ref[DN_CONV - 1:DN_CONV, sl] * xa[SUBLANES:]
        for j in range(1, DN_CONV):
            y = y + cw_ref[DN_CONV - 1 - j:DN_CONV - j, sl] * pltpu.roll(xa, j, 0)[SUBLANES:]
        return _silu(y)

    def l2n(t):
        t2 = t * t
        ra = lax.rsqrt(jnp.sum(t2[:, :LANES], axis=-1, keepdims=True) + EPS)
        rb = lax.rsqrt(jnp.sum(t2[:, LANES:], axis=-1, keepdims=True) + EPS)
        return jnp.concatenate([t[:, :LANES] * ra, t[:, LANES:] * rb], axis=1)

    n_pairs = H // 2
    chains = [(bi, p) for bi in range(DN_NB) for p in range(n_pairs)]
    decays = []
    for bi in range(DN_NB):
        ab = ab_ref[bi].astype(f32)
        x = ab + dtb_ref[...]
        softplus = jnp.maximum(x, 0.0) + jnp.log1p(jnp.exp(-jnp.abs(x)))
        gc = _cumsum_rows(-jnp.exp(alog_ref[...]) * softplus)
        decays.append((gc, gc.T, _sigmoid(ab)))
    pr = []
    for bi, p in chains:
        gc, gc_t, beta_t = decays[bi]
        ha, hb = 2 * p, 2 * p + 1
        q = l2n(conv_silu(bi, p * P2)) * (DK ** -0.5)
        k = l2n(conv_silu(bi, D + p * P2))
        v = conv_silu(bi, 2 * D + p * P2)
        gcb = _pair_cols(gc, ha, hb, C)
        betab = _pair_cols(beta_t, H + ha, H + hb, C)
        gcr = jnp.concatenate([jnp.broadcast_to(gc_t[ha:ha + 1, :], (C, LANES)),
                               jnp.broadcast_to(gc_t[hb:hb + 1, :], (C, LANES))], axis=1)
        decay = jnp.exp(jnp.where(tri, gcb - gcr, -1e30))
        eg = jnp.exp(gcb)
        gl = jnp.concatenate([jnp.broadcast_to(gc[C - 1:C, ha:ha + 1], (C, LANES)),
                              jnp.broadcast_to(gc[C - 1:C, hb:hb + 1], (C, LANES))], axis=1)
        kb = k * betab
        vb = v * betab
        gram = _dot_nt(jnp.concatenate([kb, q], axis=0).astype(bf16),
                       _block_diag_rows(k.astype(bf16)))
        kbe = kb * eg
        rhs = jnp.concatenate([jnp.concatenate([vb[:, :LANES], kbe[:, :LANES]], axis=1),
                               jnp.concatenate([vb[:, LANES:], kbe[:, LANES:]], axis=1)], axis=0).astype(bf16)
        pr.append(dict(
            a16=_block_diag_rows(jnp.where(strict, gram[:C] * decay, 0.0).astype(bf16)),
            intra=jnp.where(tri, gram[C:] * decay, 0.0).astype(bf16),
            rhs=rhs, qe=q * eg, k_dec=(k * jnp.exp(gl - gcb)).astype(bf16),
            dec_rows=jnp.concatenate(
                [jnp.broadcast_to(jnp.exp(gc[C - 1:C, ha:ha + 1]), (LANES, P2)),
                 jnp.broadcast_to(jnp.exp(gc[C - 1:C, hb:hb + 1]), (LANES, P2))], axis=0)))

    eye, pair_mask, level_masks = masks
    level_sel = [jnp.where(m, 1.0, 0.0).astype(bf16) for m in level_masks]
    eye16 = jnp.where(eye, 1.0, 0.0).astype(bf16)
    pair_sel = jnp.where(pair_mask, 1.0, 0.0).astype(bf16)
    a16s = [d["a16"] for d in pr]
    ts = [eye16 - a16 * pair_sel for a16 in a16s]
    for sel in level_sel:
        inner = [_dot(a16 * sel, t).astype(bf16) for a16, t in zip(a16s, ts)]
        ts = [t - _dot(t, x).astype(bf16) for t, x in zip(ts, inner)]

    sols = [_dot(t, d["rhs"]) for t, d in zip(ts, pr)]
    us = [jnp.concatenate([s[:C, :LANES], s[C:, :LANES]], axis=1) for s in sols]
    ws = [jnp.concatenate([s[:C, LANES:], s[C:, LANES:]], axis=1) for s in sols]
    s2s = [state_ref[n] for n in range(len(chains))]
    wqs = [_dot(jnp.concatenate([w, d["qe"]], axis=0).astype(bf16), s2.astype(bf16))
           for w, d, s2 in zip(ws, pr, s2s)]
    v_news = [u - wq[:C] for u, wq in zip(us, wqs)]
    vn16s = [vn.astype(bf16) for vn in v_news]
    outs = [wq[C:] + _dot(d["intra"], _block_diag_rows(vn16)) for wq, d, vn16 in zip(wqs, pr, vn16s)]
    kvs = [_dot_tn(d["k_dec"], vn16) for d, vn16 in zip(pr, vn16s)]
    for n, (bi, p) in enumerate(chains):
        state_ref[n] = s2s[n] * pr[n]["dec_rows"] + jnp.where(same_head, kvs[n], 0.0)
        o = outs[n]
        o2 = o * o
        ra = lax.rsqrt(jnp.mean(o2[:, :LANES], axis=-1, keepdims=True) + EPS)
        rb = lax.rsqrt(jnp.mean(o2[:, LANES:], axis=-1, keepdims=True) + EPS)
        on = jnp.concatenate([o[:, :LANES] * ra * nw_ref[...], o[:, LANES:] * rb * nw_ref[...]], axis=1)
        zz = z_ref[bi, :, p * P2:(p + 1) * P2].astype(f32)
        o_ref[bi, :, p * P2:(p + 1) * P2] = (on * _silu(zz)).astype(o_ref.dtype)


def _deltanet(proj, conv_w, a_log, dt_bias, norm_w):
    B, S, _ = proj.shape
    C = DN_C
    pad = lambda t: jnp.pad(t.reshape(1, H), ((0, 0), (0, LANES - H)))
    col = lambda idx, width: pl.BlockSpec((DN_NB, C, width), lambda b, c: (b, c, idx))
    const = lambda shape: pl.BlockSpec(shape, lambda b, c: (0,) * len(shape))
    return pl.pallas_call(
        _deltanet_kernel,
        grid=(B // DN_NB, S // C),
        in_specs=[col(0, D), col(1, D), col(2, D), col(P_Z // D, D), col(P_AB // LANES, LANES),
                  const((DN_CONV, 3 * D)), const((1, LANES)), const((1, LANES)), const((1, DK))],
        out_specs=pl.BlockSpec((DN_NB, C, D), lambda b, c: (b, c, 0)),
        out_shape=jax.ShapeDtypeStruct((B, S, D), bf16),
        scratch_shapes=[pltpu.VMEM((DN_NB, C + SUBLANES, 3 * D), f32),
                        pltpu.VMEM((DN_NB * (H // 2), 2 * LANES, 2 * LANES), f32)],
        compiler_params=_cparams(2),
        name="deltanet",
    )(proj, proj, proj, proj, proj, conv_w, pad(a_log), pad(dt_bias), norm_w.reshape(1, DK))


def _rope(r, cos, sin_lo, sin_hi):
    return r * cos + pltpu.roll(r, LANES - ROPE // 2, 1) * sin_lo + pltpu.roll(r, ROPE // 2, 1) * sin_hi


def _mlaprep_kernel(qt_ref, ckv_ref, kr_ref, pos_ref, inv_ref, kvw_ref, qw_ref,
                    kwn_ref, kwr_ref, wuk_ref, wuvt_ref, qo_ref, ko_ref, vo_ref):
    tm = ckv_ref.shape[1]
    half = ROPE // 2

    ang_t = _lane_tile(inv_ref[...], tm) * pos_ref[0].astype(f32)
    cos_t = jnp.cos(ang_t)
    sin_t = jnp.sin(ang_t)
    z_half = jnp.zeros((half, tm), f32)
    z_pad = jnp.zeros((LANES - ROPE, tm), f32)
    cos = jnp.concatenate([cos_t, cos_t, z_pad], axis=0).T
    sin_lo = jnp.concatenate([-sin_t, z_half, z_pad], axis=0).T
    sin_hi = jnp.concatenate([z_half, sin_t, z_pad], axis=0).T

    ckv = ckv_ref[0].astype(f32)
    ckv_n = (ckv * lax.rsqrt(jnp.mean(ckv * ckv, axis=-1, keepdims=True) + EPS) * kvw_ref[...]).astype(bf16)
    k_nope = _dot(ckv_n, wuk_ref[...])
    vo_ref[0] = _dot_nt(wuvt_ref[...], ckv_n).astype(vo_ref.dtype)

    kr = kr_ref[0].astype(f32)
    kr_ss = jnp.sum(kr * kr, axis=-1, keepdims=True)
    kr_rot = _rope(kr * kwr_ref[...], cos, sin_lo, sin_hi)
    for h in range(H):
        kn = k_nope[:, h * NOPE:(h + 1) * NOPE]
        rk = lax.rsqrt((jnp.sum(kn * kn, axis=-1, keepdims=True) + kr_ss) * (1.0 / QK_DIM) + EPS)
        ko_ref[0, :, h * QPAD:h * QPAD + NOPE] = (kn * rk * kwn_ref[...]).astype(ko_ref.dtype)
        ko_ref[0, :, h * QPAD + NOPE:(h + 1) * QPAD] = (kr_rot * rk).astype(ko_ref.dtype)

    qw_t = _lane_tile(qw_ref[...], tm)
    for h in range(H):
        qh = qt_ref[0, h * QK_DIM:(h + 1) * QK_DIM, :].astype(f32)
        rq = lax.rsqrt(jnp.sum(qh * qh, axis=0, keepdims=True) * (1.0 / QK_DIM) + EPS) * (QK_DIM ** -0.5 * LOG2E)
        qn = qh * rq * qw_t
        x1 = qn[NOPE:NOPE + half]
        x2 = qn[NOPE + half:QK_DIM]
        r0 = h * QPAD
        qo_ref[0, r0:r0 + NOPE, :] = qn[:NOPE].astype(qo_ref.dtype)
        qo_ref[0, r0 + NOPE:r0 + NOPE + half, :] = (x1 * cos_t - x2 * sin_t).astype(qo_ref.dtype)
        qo_ref[0, r0 + NOPE + half:r0 + QK_DIM, :] = (x2 * cos_t + x1 * sin_t).astype(qo_ref.dtype)
        qo_ref[0, r0 + QK_DIM:r0 + QPAD, :] = jnp.zeros((QPAD - QK_DIM, tm), qo_ref.dtype)


def _mlaprep(proj, q_t, positions, kv_norm_w, w_uk, w_uv, q_norm_w, k_norm_w):
    B, S, _ = proj.shape
    tm = TM_PREP
    half = ROPE // 2
    inv = ROPE_THETA ** (-jnp.arange(half, dtype=f32) / half)
    inv_c = jnp.broadcast_to(inv.reshape(half, 1), (half, LANES))
    qw = jnp.broadcast_to(q_norm_w.reshape(QK_DIM, 1), (QK_DIM, LANES))
    kwn = k_norm_w[:NOPE].reshape(1, NOPE)
    kwr = jnp.pad(k_norm_w[NOPE:], (0, LANES - ROPE)).reshape(1, LANES)
    col = lambda idx, width: pl.BlockSpec((1, tm, width), lambda b, i: (b, i, idx))
    const = lambda shape: pl.BlockSpec(shape, lambda b, i: (0,) * len(shape))
    return pl.pallas_call(
        _mlaprep_kernel,
        grid=(B, S // tm),
        in_specs=[pl.BlockSpec((1, H * QK_DIM, tm), lambda b, i: (b, 0, i)),
                  col(P_CKV // KV_RANK, KV_RANK), col(P_KR // LANES, LANES),
                  pl.BlockSpec((1, 1, tm), lambda b, i: (b, 0, i)),
                  const((half, LANES)), const((1, KV_RANK)), const((QK_DIM, LANES)),
                  const((1, NOPE)), const((1, LANES)),
                  const((KV_RANK, H * NOPE)), const((H * DV, KV_RANK))],
        out_specs=[pl.BlockSpec((1, H * QPAD, tm), lambda b, i: (b, 0, i)),
                   pl.BlockSpec((1, tm, H * QPAD), lambda b, i: (b, i, 0)),
                   pl.BlockSpec((1, H * DV, tm), lambda b, i: (b, 0, i))],
        out_shape=[jax.ShapeDtypeStruct((B, H * QPAD, S), bf16),
                   jax.ShapeDtypeStruct((B, S, H * QPAD), bf16),
                   jax.ShapeDtypeStruct((B, H * DV, S), bf16)],
        compiler_params=_cparams(2),
        name="mlaprep",
    )(q_t, proj, proj, positions.reshape(B, 1, S), inv_c,
      kv_norm_w.reshape(1, KV_RANK), qw, kwn, kwr, w_uk.astype(bf16), w_uv.T.astype(bf16))


def _attn_kernel(q_ref, k_ref, v_ref, o_ref, m_ref, acc_ref, s_ref):
    G = ATT_G
    i = pl.program_id(2)
    last = ((i + 1) * TQ - 1) // TK

    def qk(j, slot):
        start = pl.multiple_of(j * TK, TK)
        for g in range(G):
            s_ref[slot, g] = _dot(k_ref[0, pl.ds(start, TK), g * QPAD:(g + 1) * QPAD],
                                  q_ref[0, g * QPAD:(g + 1) * QPAD, :])

    def softmax_pv(j, slot, masked=False, first=False):
        start = pl.multiple_of(j * TK, TK)
        if masked:
            kc = (lax.broadcasted_iota(jnp.int32, (TK, TQ), 0) + j * TK) // CHUNK
            qc = (lax.broadcasted_iota(jnp.int32, (TK, TQ), 1) + i * TQ) // CHUNK
            visible = qc >= kc
        probs, alphas = [], []
        for g in range(G):
            s = s_ref[slot, g]
            if masked:
                s = jnp.where(visible, s, -jnp.inf)
            m_new = jnp.max(s, axis=0, keepdims=True)
            if not first:
                m_prev = m_ref[g]
                m_new = jnp.maximum(m_prev, m_new)
                alphas.append(jnp.exp2(m_prev - m_new))
            probs.append(jnp.exp2(s - m_new).astype(bf16))
            m_ref[g] = m_new
        ones = jnp.ones((DV, TK), bf16)
        for g in range(G):
            v1 = jnp.concatenate([v_ref[0, g * DV:(g + 1) * DV, pl.ds(start, TK)], ones], axis=0)
            pv = _dot(v1, probs[g])
            acc_ref[g] = pv if first else alphas[g] * acc_ref[g] + pv

    qk(last, 1)

    @pl.when(last > 0)
    def _():
        qk(0, 0)

    softmax_pv(last, 1, masked=True, first=True)

    def body(p, carry):
        qk(2 * p + 1, 1)
        softmax_pv(2 * p, 0)
        qk(2 * p + 2, 0)
        softmax_pv(2 * p + 1, 1)
        return carry

    lax.fori_loop(0, (last - 1) // 2, body, 0)

    @pl.when((last > 0) & (last % 2 == 1))
    def _():
        softmax_pv(last - 1, 0)

    @pl.when((last > 0) & (last % 2 == 0))
    def _():
        qk(last - 1, 1)
        softmax_pv(last - 2, 0)
        softmax_pv(last - 1, 1)

    for g in range(G):
        acc = acc_ref[g]
        o_ref[0, :, g * DV:(g + 1) * DV] = (acc[:DV] / acc[DV:]).T.astype(o_ref.dtype)


def _attn(q_t, kp, v_t):
    B, S, _ = kp.shape
    G = ATT_G
    return pl.pallas_call(
        _attn_kernel,
        grid=(B, H // G, S // TQ),
        in_specs=[pl.BlockSpec((1, G * QPAD, TQ), lambda b, h, i: (b, h, i)),
                  pl.BlockSpec((1, S, G * QPAD), lambda b, h, i: (b, 0, h)),
                  pl.BlockSpec((1, G * DV, S), lambda b, h, i: (b, h, 0))],
        out_specs=pl.BlockSpec((1, TQ, G * DV), lambda b, h, i: (b, i, h)),
        out_shape=jax.ShapeDtypeStruct((B, S, H * DV), bf16),
        scratch_shapes=[pltpu.VMEM((G, 1, TQ), f32), pltpu.VMEM((G, 2 * DV, TQ), f32),
                        pltpu.VMEM((2, G, TK, TQ), f32)],
        compiler_params=_cparams(3),
        name="attn",
    )(q_t, kp, v_t)


def _merge_kernel(x_ref, oa_ref, ob_ref, gd_ref, gm_ref, mod_ref, wa_ref, wb_ref, wo_ref, o_ref):
    y_a = _dot(oa_ref[0], wa_ref[...])
    y_b = _dot(ob_ref[0], wb_ref[...])
    mix = _sigmoid(gd_ref[0].astype(f32)) * y_a + _sigmoid(gm_ref[0].astype(f32)) * y_b
    o_ref[0] = x_ref[0] + mod_ref[0, 2:3, :] * _dot(mix.astype(bf16), wo_ref[...])


def _merge(x, o_a, o_b, proj, mod3, w_a, w_b, w_o):
    B, S, _ = x.shape
    tm = TM_MERGE
    row = lambda idx: pl.BlockSpec((1, tm, D), lambda b, i: (b, i, idx))
    wspec = pl.BlockSpec((D, D), lambda b, i: (0, 0))
    return pl.pallas_call(
        _merge_kernel,
        grid=(B, S // tm),
        in_specs=[row(0), row(0), row(0), row(P_GDN // D), row(P_GMLA // D),
                  pl.BlockSpec((1, 6, D), lambda b, i: (b, 0, 0)), wspec, wspec, wspec],
        out_specs=row(0),
        out_shape=jax.ShapeDtypeStruct((B, S, D), f32),
        compiler_params=_cparams(2),
        name="merge",
    )(x, o_a, o_b, proj, proj, mod3, w_a.astype(bf16), w_b.astype(bf16), w_o.astype(bf16))


def _ffn_kernel(x_ref, halo_ref, mod_ref, nw_ref, wa_ref, wv_ref, cw_ref, cb_ref, wd_ref, o_ref,
                h_ref, a_ref, acc_ref):
    tm = x_ref.shape[1]
    i = pl.program_id(1)
    j = pl.program_id(2)

    @pl.when(j == 0)
    def _():
        shift, scale = mod_ref[0, 3:4, :], mod_ref[0, 4:5, :]
        hh = _modulated_norm(halo_ref[0], nw_ref[...], shift, scale)
        h_ref[0:SUBLANES, :] = jnp.where(i > 0, hh, 0.0).astype(bf16)
        h_ref[SUBLANES:, :] = _modulated_norm(x_ref[0], nw_ref[...], shift, scale).astype(bf16)
        acc_ref[...] = jnp.zeros_like(acc_ref)

    a_ref[...] = _dot(h_ref[...], wa_ref[...])
    v = _dot(h_ref[SUBLANES:, :], wv_ref[...])
    a = cb_ref[...] + cw_ref[FFN_CONV - 1:FFN_CONV, :] * a_ref[SUBLANES:SUBLANES + tm, :]
    for t in range(1, FFN_CONV):
        a = a + cw_ref[FFN_CONV - 1 - t:FFN_CONV - t, :] * a_ref[SUBLANES - t:SUBLANES - t + tm, :]
    gelu = 0.5 * a * (1.0 + lax.erf(a * (2.0 ** -0.5)))
    acc_ref[...] += _dot((gelu * v).astype(bf16), wd_ref[...])

    @pl.when(j == pl.num_programs(2) - 1)
    def _():
        o_ref[0] = x_ref[0] + mod_ref[0, 5:6, :] * acc_ref[...]


def _ffn(x, mod3, norm_w, w_up, conv_w, conv_b, w_down):
    B, S, _ = x.shape
    tm, tf = TM_FFN, TF_FFN
    nf = D_FF // tf
    hb = tm // SUBLANES
    return pl.pallas_call(
        _ffn_kernel,
        grid=(B, S // tm, nf),
        in_specs=[pl.BlockSpec((1, tm, D), lambda b, i, j: (b, i, 0)),
                  pl.BlockSpec((1, SUBLANES, D), lambda b, i, j: (b, jnp.maximum(i * hb - 1, 0), 0)),
                  pl.BlockSpec((1, 6, D), lambda b, i, j: (b, 0, 0)),
                  pl.BlockSpec((1, D), lambda b, i, j: (0, 0)),
                  pl.BlockSpec((D, tf), lambda b, i, j: (0, j)),
                  pl.BlockSpec((D, tf), lambda b, i, j: (0, nf + j)),
                  pl.BlockSpec((FFN_CONV, tf), lambda b, i, j: (0, j)),
                  pl.BlockSpec((1, tf), lambda b, i, j: (0, j)),
                  pl.BlockSpec((tf, D), lambda b, i, j: (j, 0))],
        out_specs=pl.BlockSpec((1, tm, D), lambda b, i, j: (b, i, 0)),
        out_shape=jax.ShapeDtypeStruct((B, S, D), f32),
        scratch_shapes=[pltpu.VMEM((tm + SUBLANES, D), bf16),
                        pltpu.VMEM((tm + SUBLANES, tf), f32),
                        pltpu.VMEM((tm, D), f32)],
        compiler_params=_cparams(3),
        name="ffn",
    )(x, x, mod3, norm_w, w_up, w_up, conv_w, conv_b.reshape(1, D_FF), w_down)


def kernel(x, c, positions, w_ada, b_ada, norm1_w, w_in, dn_conv_w, dn_a_log, dn_dt_bias, dn_norm_w,
           mla_kv_norm_w, mla_w_uk, mla_w_uv, mla_q_norm_w, mla_k_norm_w, w_out_dn, w_out_mla, w_o,
           norm2_w, ffn_w_up, ffn_conv_w, ffn_conv_b, ffn_w_down):
    B = x.shape[0]
    for l in range(w_ada.shape[0]):
        mod3 = _mod(c, w_ada[l], b_ada[l]).reshape(B, 6, D)
        w_p, w_qt = _pack_w_in(w_in[l])
        proj, q_t = _inproj(x, mod3, norm1_w[l].reshape(1, D), w_p, w_qt)
        o_a = _deltanet(proj, dn_conv_w[l], dn_a_log[l], dn_dt_bias[l], dn_norm_w[l])
        qp_t, kp, v_t = _mlaprep(proj, q_t, positions, mla_kv_norm_w[l], mla_w_uk[l], mla_w_uv[l],
                                 mla_q_norm_w[l], mla_k_norm_w[l])
        o_b = _attn(qp_t, kp, v_t)
        x = _merge(x, o_a, o_b, proj, mod3, w_out_dn[l], w_out_mla[l], w_o[l])
        x = _ffn(x, mod3, norm2_w[l].reshape(1, D), ffn_w_up[l].astype(bf16), ffn_conv_w[l],
                 ffn_conv_b[l], ffn_w_down[l].astype(bf16))
    return x
```
